```python
import math
import jax, jax.numpy as jnp
from jax import lax
import numpy as np

D_MODEL = 1024
BATCH = 1
SEQ = 16384
DEPTH = 1
DEC_BATCH = 32
DEC_SEQ = 2048
PAST_LEN = 128

MIX_WIDTH = D_MODEL
GLA_HEADS = 4
GLA_DV = MIX_WIDTH // 2 // GLA_HEADS
GLA_DK = GLA_DV // 2
GLA_QK = GLA_HEADS * GLA_DK
GLA_V = GLA_HEADS * GLA_DV
GLA_GATE_RANK = 16
GLA_GATE_NORMALIZER = 16.0
GLA_CHUNK = 64
DIFF_HEADS = 4
DIFF_DH = MIX_WIDTH // 2 // DIFF_HEADS // 2
DIFF_QK = DIFF_HEADS * 2 * DIFF_DH
DIFF_V = DIFF_HEADS * 2 * DIFF_DH
ROT_DIM = DIFF_DH // 4
ROPE_THETA = 500000.0
Q_BLOCK = 128
MEM_LEN = 256
XATTN_HEADS = 4
XATTN_DH = D_MODEL // XATTN_HEADS
D_FF = ((8 * D_MODEL + 3 * 256 - 1) // (3 * 256)) * 256
RMS_EPS = 1e-6
IN_SPLIT_SIZES = (GLA_QK, GLA_QK, GLA_V, GLA_GATE_RANK, GLA_GATE_RANK, GLA_V, DIFF_QK, DIFF_QK, DIFF_V)
IN_COLS = GLA_QK * 2 + GLA_V * 2 + GLA_GATE_RANK * 2 + DIFF_QK * 2 + DIFF_V

kernel_name = "hybrid_gla_diffattn_encoder"


def rmsnorm(x, w):
    xf = x.astype(jnp.float32)
    y = xf * lax.rsqrt(jnp.mean(xf * xf, axis=-1, keepdims=True) + RMS_EPS)
    return (y * w.astype(jnp.float32)).astype(x.dtype)


def rope_partial(t, pos):
    inv = ROPE_THETA ** (-jnp.arange(0, ROT_DIM, 2, dtype=jnp.float32) / ROT_DIM)
    ang = pos.astype(jnp.float32)[:, None] * inv[None, :]
    cos = jnp.cos(ang)[:, None, None, :]
    sin = jnp.sin(ang)[:, None, None, :]
    tf = t.astype(jnp.float32)
    half = ROT_DIM // 2
    x1 = tf[..., :half]
    x2 = tf[..., half:ROT_DIM]
    rest = tf[..., ROT_DIM:]
    return jnp.concatenate([x1 * cos - x2 * sin, x2 * cos + x1 * sin, rest], axis=-1).astype(t.dtype)


def gla_chunk_scan(q, k, v, g):
    B, S, H, dk = q.shape
    dv = v.shape[-1]
    C = GLA_CHUNK
    N = S // C

    def to_chunks(t):
        return t.reshape(B, N, C, H, t.shape[-1]).transpose(0, 3, 1, 2, 4)

    q, k, v, g = to_chunks(q), to_chunks(k), to_chunks(v), to_chunks(g)
    b = jnp.cumsum(g, axis=3)
    b_last = b[:, :, :, -1:, :]
    q_e = q * jnp.exp(b)
    k_e = k * jnp.exp(-b)
    k_d = k * jnp.exp(b_last - b)
    mask = jnp.tril(jnp.ones((C, C), dtype=bool))
    A = jnp.where(mask, jnp.einsum('bhnid,bhnjd->bhnij', q_e, k_e), 0.0)
    o_intra = jnp.einsum('bhnij,bhnjv->bhniv', A, v)
    U = jnp.einsum('bhncd,bhncv->bhndv', k_d, v)
    decay = jnp.exp(b_last[:, :, :, 0, :])

    def step(state, inp):
        dec, u = inp
        return dec[..., None] * state + u, state

    _, s_prev = lax.scan(step, jnp.zeros((B, H, dk, dv), jnp.float32),
                         (decay.transpose(2, 0, 1, 3), U.transpose(2, 0, 1, 3, 4)))
    s_prev = s_prev.transpose(1, 2, 0, 3, 4)
    o_inter = jnp.einsum('bhncd,bhndv->bhncv', q_e, s_prev)
    return (o_intra + o_inter).transpose(0, 2, 3, 1, 4).reshape(B, S, H, dv)


def gla_mixer(q, k, v, gf, gb, og, w_gate_up_f, b_gate_f, w_gate_up_b, b_gate_b, gla_norm_w):
    B, S, _ = q.shape
    dt = q.dtype
    f32 = jnp.float32
    q = q.astype(f32).reshape(B, S, GLA_HEADS, GLA_DK) * (GLA_DK ** -0.5)
    k = k.astype(f32).reshape(B, S, GLA_HEADS, GLA_DK)
    v = v.astype(f32).reshape(B, S, GLA_HEADS, GLA_DV)
    g_f = jax.nn.log_sigmoid(gf.astype(f32) @ w_gate_up_f.astype(f32) + b_gate_f.astype(f32)) / GLA_GATE_NORMALIZER
    g_b = jax.nn.log_sigmoid(gb.astype(f32) @ w_gate_up_b.astype(f32) + b_gate_b.astype(f32)) / GLA_GATE_NORMALIZER
    g_f = g_f.reshape(B, S, GLA_HEADS, GLA_DK)
    g_b = g_b.reshape(B, S, GLA_HEADS, GLA_DK)
    o_f = gla_chunk_scan(q, k, v, g_f)
    flip = lambda t: jnp.flip(t, axis=1)
    o_b = flip(gla_chunk_scan(flip(q), flip(k), flip(v), flip(g_b)))
    o = rmsnorm(o_f + o_b, gla_norm_w)
    o = o * jax.nn.silu(og.astype(f32).reshape(B, S, GLA_HEADS, GLA_DV))
    return o.reshape(B, S, GLA_V).astype(dt)


def diff_attention(q, k, v, lam, lam_init, subln_w, pos):
    B, S, _ = q.shape
    q = rope_partial(q.reshape(B, S, DIFF_HEADS, 2, DIFF_DH), pos) * (DIFF_DH ** -0.5)
    k = rope_partial(k.reshape(B, S, DIFF_HEADS, 2, DIFF_DH), pos)
    v = v.reshape(B, S, DIFF_HEADS, 2 * DIFF_DH)
    nb = S // Q_BLOCK
    qb = q.reshape(B, nb, Q_BLOCK, DIFF_HEADS, 2, DIFF_DH).transpose(1, 0, 2, 3, 4, 5)

    def block(qi):
        s = jnp.einsum('bqhcd,bkhcd->bhcqk', qi, k).astype(jnp.float32)
        p = jax.nn.softmax(s, axis=-1)
        p = p[:, :, 0] - lam * p[:, :, 1]
        return jnp.einsum('bhqk,bkhe->bqhe', p.astype(v.dtype), v)

    o = lax.map(block, qb)
    o = o.transpose(1, 0, 2, 3, 4).reshape(B, S, DIFF_HEADS, 2 * DIFF_DH)
    o = rmsnorm(o, subln_w) * (1.0 - lam_init)
    return o.reshape(B, S, DIFF_V)


def memory_cross_attention(h, m, w_xq, w_xkv, w_xo):
    B, S, _ = h.shape
    M = m.shape[1]
    q = (h @ w_xq).reshape(B, S, XATTN_HEADS, XATTN_DH)
    kv = m @ w_xkv
    k = kv[..., :D_MODEL].reshape(B, M, XATTN_HEADS, XATTN_DH)
    v = kv[..., D_MODEL:].reshape(B, M, XATTN_HEADS, XATTN_DH)
    s = jnp.einsum('bshd,bmhd->bhsm', q, k).astype(jnp.float32) * (XATTN_DH ** -0.5)
    p = jax.nn.softmax(s, axis=-1)
    o = jnp.einsum('bhsm,bmhd->bshd', p.astype(v.dtype), v).reshape(B, S, D_MODEL)
    return o @ w_xo


def swiglu(h, w_gate, w_up, w_down):
    return (jax.nn.silu(h @ w_gate) * (h @ w_up)) @ w_down


def encoder_layer(x, mem, p, lam_init):
    B, S, _ = x.shape
    pos = jnp.arange(S)
    h = rmsnorm(x, p['norm_mix_pre'])
    proj = h @ p['w_in']
    idx = []
    acc = 0
    for sz in IN_SPLIT_SIZES[:-1]:
        acc += sz
        idx.append(acc)
    gq, gk, gv, gf, gb, og, dq, dk, dv = jnp.split(proj, idx, axis=-1)
    o_gla = gla_mixer(gq, gk, gv, gf, gb, og, p['w_gate_up_f'], p['b_gate_f'],
                      p['w_gate_up_b'], p['b_gate_b'], p['gla_norm_w'])
    f32 = jnp.float32
    lam = (jnp.exp(jnp.sum(p['lambda_q1'].astype(f32) * p['lambda_k1'].astype(f32)))
           - jnp.exp(jnp.sum(p['lambda_q2'].astype(f32) * p['lambda_k2'].astype(f32))) + lam_init)
    o_diff = diff_attention(dq, dk, dv, lam, lam_init, p['diff_subln_w'], pos)
    mix = jnp.concatenate([o_gla, o_diff], axis=-1) @ p['w_out']
    x = x + rmsnorm(mix, p['norm_mix_post'])
    h = rmsnorm(x, p['norm_xattn_pre'])
    m = rmsnorm(mem, p['norm_mem'])
    x = x + rmsnorm(memory_cross_attention(h, m, p['w_xq'], p['w_xkv'], p['w_xo']), p['norm_xattn_post'])
    h = rmsnorm(x, p['norm_ffn_pre'])
    x = x + rmsnorm(swiglu(h, p['w_ffn_gate'], p['w_ffn_up'], p['w_ffn_down']), p['norm_ffn_post'])
    return x


def setup_inputs(seed: int = 0) -> dict:
    key = jax.random.key(seed)
    keys = jax.random.split(key, 40)
    it = iter(range(40))
    L = DEPTH

    def nrm(shape, scale):
        return scale * jax.random.normal(keys[next(it)], shape, jnp.float32)

    def gain(shape):
        return 1.0 + 0.02 * jax.random.normal(keys[next(it)], shape, jnp.float32)

    return {
        "x_prompt": nrm((BATCH, SEQ, D_MODEL), 1.0),
        "x_sample": nrm((DEC_BATCH, DEC_SEQ, D_MODEL), 1.0),
        "mem_prompt": nrm((BATCH, MEM_LEN, D_MODEL), 1.0),
        "mem_sample": nrm((DEC_BATCH, MEM_LEN, D_MODEL), 1.0),
        "norm_mix_pre": gain((L, D_MODEL)),
        "w_in": nrm((L, D_MODEL, IN_COLS), D_MODEL ** -0.5),
        "w_gate_up_f": nrm((L, GLA_GATE_RANK, GLA_QK), GLA_GATE_RANK ** -0.5),
        "b_gate_f": nrm((L, GLA_QK), 0.1),
        "w_gate_up_b": nrm((L, GLA_GATE_RANK, GLA_QK), GLA_GATE_RANK ** -0.5),
        "b_gate_b": nrm((L, GLA_QK), 0.1),
        "gla_norm_w": gain((L, GLA_DV)),
        "lambda_q1": nrm((L, DIFF_DH), 0.1),
        "lambda_k1": nrm((L, DIFF_DH), 0.1),
        "lambda_q2": nrm((L, DIFF_DH), 0.1),
        "lambda_k2": nrm((L, DIFF_DH), 0.1),
        "diff_subln_w": gain((L, 2 * DIFF_DH)),
        "w_out": nrm((L, MIX_WIDTH, D_MODEL), MIX_WIDTH ** -0.5),
        "norm_mix_post": gain((L, D_MODEL)),
        "norm_xattn_pre": gain((L, D_MODEL)),
        "norm_mem": gain((L, D_MODEL)),
        "w_xq": nrm((L, D_MODEL, D_MODEL), D_MODEL ** -0.5),
        "w_xkv": nrm((L, D_MODEL, 2 * D_MODEL), D_MODEL ** -0.5),
        "w_xo": nrm((L, D_MODEL, D_MODEL), D_MODEL ** -0.5),
        "norm_xattn_post": gain((L, D_MODEL)),
        "norm_ffn_pre": gain((L, D_MODEL)),
        "w_ffn_gate": nrm((L, D_MODEL, D_FF), D_MODEL ** -0.5),
        "w_ffn_up": nrm((L, D_MODEL, D_FF), D_MODEL ** -0.5),
        "w_ffn_down": nrm((L, D_FF, D_MODEL), D_FF ** -0.5),
        "norm_ffn_post": gain((L, D_MODEL)),
    }


def reference(x_prompt, x_sample, mem_prompt, mem_sample, norm_mix_pre, w_in, w_gate_up_f, b_gate_f,
              w_gate_up_b, b_gate_b, gla_norm_w, lambda_q1, lambda_k1, lambda_q2, lambda_k2, diff_subln_w,
              w_out, norm_mix_post, norm_xattn_pre, norm_mem, w_xq, w_xkv, w_xo, norm_xattn_post,
              norm_ffn_pre, w_ffn_gate, w_ffn_up, w_ffn_down, norm_ffn_post):
    y_prompt = x_prompt
    y_sample = x_sample
    for l in range(DEPTH):
        lam_init = 0.8 - 0.6 * math.exp(-0.3 * l)
        p = {
            'norm_mix_pre': norm_mix_pre[l], 'w_in': w_in[l],
            'w_gate_up_f': w_gate_up_f[l], 'b_gate_f': b_gate_f[l],
            'w_gate_up_b': w_gate_up_b[l], 'b_gate_b': b_gate_b[l],
            'gla_norm_w': gla_norm_w[l],
            'lambda_q1': lambda_q1[l], 'lambda_k1': lambda_k1[l],
            'lambda_q2': lambda_q2[l], 'lambda_k2': lambda_k2[l],
            'diff_subln_w': diff_subln_w[l], 'w_out': w_out[l], 'norm_mix_post': norm_mix_post[l],
            'norm_xattn_pre': norm_xattn_pre[l], 'norm_mem': norm_mem[l],
            'w_xq': w_xq[l], 'w_xkv': w_xkv[l], 'w_xo': w_xo[l], 'norm_xattn_post': norm_xattn_post[l],
            'norm_ffn_pre': norm_ffn_pre[l], 'w_ffn_gate': w_ffn_gate[l], 'w_ffn_up': w_ffn_up[l],
            'w_ffn_down': w_ffn_down[l], 'norm_ffn_post': norm_ffn_post[l],
        }
        y_prompt = encoder_layer(y_prompt, mem_prompt, p, lam_init)
        y_sample = encoder_layer(y_sample, mem_sample, p, lam_init)
    return (y_prompt, y_sample)
```

```python
import functools
import math

import numpy as np
import jax
import jax.numpy as jnp
from jax import lax
from jax.experimental import pallas as pl
from jax.experimental.pallas import tpu as pltpu

F32 = jnp.float32
BF16 = jnp.bfloat16

D_MODEL = 1024
GLA_HEADS = 4
GLA_DK = 64
GLA_DV = 128
GLA_QK = GLA_HEADS * GLA_DK
GLA_V = GLA_HEADS * GLA_DV
GLA_GATE_RANK = 16
GLA_GATE_NORMALIZER = 16.0
GLA_CHUNK = 64
DIFF_HEADS = 4
DIFF_DH = 64
DIFF_QK = DIFF_HEADS * 2 * DIFF_DH
DIFF_V = DIFF_HEADS * 2 * DIFF_DH
ROT_DIM = DIFF_DH // 4
ROPE_THETA = 500000.0
MEM_LEN = 256
XATTN_HEADS = 4
XATTN_DH = D_MODEL // XATTN_HEADS
D_FF = 2816
RMS_EPS = 1e-6

LANES = 128
TOKEN_TILE = 512
GLA_BLOCK = 512
ATTN_Q_TILE = 512
ATTN_K_TILE = 512
VMEM_LIMIT = 56 * 1024 * 1024

NT_DIMS = (((1,), (1,)), ((), ()))
TN_DIMS = (((0,), (0,)), ((), ()))


def _rms(x, w):
    return x * lax.rsqrt(jnp.mean(x * x, axis=-1, keepdims=True) + RMS_EPS) * w


def _dot(a, b):
    return jnp.dot(a, b, preferred_element_type=F32)


def _dot_nt(a, b):
    return lax.dot_general(a, b, NT_DIMS, preferred_element_type=F32)


def _dot_tn(a, b):
    return lax.dot_general(a, b, TN_DIMS, preferred_element_type=F32)


def _const_spec(shape):
    return pl.BlockSpec(shape, lambda *_: (0,) * len(shape))


def _params(semantics):
    return pltpu.CompilerParams(dimension_semantics=semantics, vmem_limit_bytes=VMEM_LIMIT)


def _mem_kv_kernel(m_ref, nw_ref, w_ref, k_ref, v_ref):
    m = _rms(m_ref[...], nw_ref[...]).astype(BF16)
    kv = _dot(m, w_ref[...])
    k_ref[...] = kv[:, :D_MODEL].astype(BF16)
    v_ref[...] = kv[:, D_MODEL:].astype(BF16)


def _mem_kv(mem2d, norm_w, w_xkv):
    rows = mem2d.shape[0]
    out = jax.ShapeDtypeStruct((rows, D_MODEL), BF16)
    return pl.pallas_call(
        _mem_kv_kernel,
        grid=(rows // MEM_LEN,),
        in_specs=[pl.BlockSpec((MEM_LEN, D_MODEL), lambda i: (i, 0)),
                  _const_spec((1, D_MODEL)),
                  _const_spec((D_MODEL, 2 * D_MODEL))],
        out_specs=[pl.BlockSpec((MEM_LEN, D_MODEL), lambda i: (i, 0))] * 2,
        out_shape=[out, out],
        compiler_params=_params(("parallel",)),
        name="mem_kv",
    )(mem2d, norm_w, w_xkv)


def _inproj_kernel(x_ref, nw_ref, wa_ref, wg_ref, wup_ref, bg_ref, mf_ref, mb_ref, cos_ref, sin_ref,
                   f_ref, b_ref, v_ref, og_ref, decf_ref, decb_ref, dq_ref, dk_ref, dv_ref):
    tm = x_ref.shape[0]
    h = _rms(x_ref[...], nw_ref[...]).astype(BF16)

    gdown = _dot(h, wg_ref[...]).astype(BF16)
    pre = _dot(gdown, wup_ref[...]) + bg_ref[...]
    g = (jnp.minimum(pre, 0.0) - jnp.log1p(jnp.exp(-jnp.abs(pre)))) * (1.0 / GLA_GATE_NORMALIZER)
    g_hi = g.astype(BF16)
    g_lo = (g - g_hi.astype(F32)).astype(BF16)

    def chunk_sums(m_ref, lo, hi):
        m = m_ref[...]
        return _dot(m, g_hi[:, lo:hi]) + _dot(m, g_lo[:, lo:hi])

    rf = chunk_sums(mf_ref, 0, GLA_QK)
    rb = chunk_sums(mb_ref, GLA_QK, 2 * GLA_QK)

    qk = _dot(h, wa_ref[:, 0:2 * GLA_QK])
    q = qk[:, :GLA_QK] * (GLA_DK ** -0.5)
    k = qk[:, GLA_QK:]
    for r, o_ref, dec_ref in ((rf, f_ref, decf_ref), (rb, b_ref, decb_ref)):
        cum = r[0:tm]
        o_ref[:, 0:GLA_QK] = (q * jnp.exp(cum)).astype(BF16)
        o_ref[:, GLA_QK:2 * GLA_QK] = (k * jnp.exp(-cum)).astype(BF16)
        o_ref[:, 2 * GLA_QK:3 * GLA_QK] = (k * jnp.exp(r[tm:2 * tm])).astype(BF16)
        dec_ref[...] = jnp.exp(r[2 * tm:])

    off = 2 * GLA_QK
    v_ref[...] = _dot(h, wa_ref[:, off:off + GLA_V]).astype(BF16)
    off += GLA_V
    og_ref[...] = _dot(h, wa_ref[:, off:off + GLA_V]).astype(BF16)
    off += GLA_V

    cos = cos_ref[...]
    sin = sin_ref[...]
    lane = lax.broadcasted_iota(jnp.int32, (tm, LANES), 1)
    second_half = (lane & (DIFF_DH - 1)) >= (ROT_DIM // 2)

    def rope(t):
        slabs = []
        for s in range(t.shape[1] // LANES):
            ts = t[:, s * LANES:(s + 1) * LANES]
            partner = jnp.where(second_half, pltpu.roll(ts, ROT_DIM // 2, 1),
                                pltpu.roll(ts, LANES - ROT_DIM // 2, 1))
            slabs.append(ts * cos + partner * sin)
        return jnp.concatenate(slabs, axis=1)

    dq = _dot(h, wa_ref[:, off:off + DIFF_QK])
    dq_ref[...] = (rope(dq) * (DIFF_DH ** -0.5)).astype(BF16)
    off += DIFF_QK
    dk = _dot(h, wa_ref[:, off:off + DIFF_QK])
    dk_ref[...] = rope(dk).astype(BF16)
    off += DIFF_QK
    dv_ref[...] = _dot(h, wa_ref[:, off:off + DIFF_V]).astype(BF16)


def _chunk_sum_matrices(tm):
    r = np.arange(tm)[:, None]
    c = np.arange(tm)[None, :]
    same = (r // GLA_CHUNK) == (c // GLA_CHUNK)
    total = (np.arange(tm // GLA_CHUNK)[:, None] == (c // GLA_CHUNK))
    fwd = np.concatenate([same & (c <= r), same & (c > r), total], axis=0)
    bwd = np.concatenate([same & (c >= r), same & (c < r), total], axis=0)
    return jnp.asarray(fwd, BF16), jnp.asarray(bwd, BF16)


def _rope_tables(seq):
    inv = ROPE_THETA ** (-jnp.arange(0, ROT_DIM, 2, dtype=F32) / ROT_DIM)
    ang = jnp.arange(seq).astype(F32)[:, None] * inv[None, :]
    cos, sin = jnp.cos(ang), jnp.sin(ang)
    rest = DIFF_DH - ROT_DIM
    cos64 = jnp.concatenate([cos, cos, jnp.ones((seq, rest), F32)], axis=1)
    sin64 = jnp.concatenate([-sin, sin, jnp.zeros((seq, rest), F32)], axis=1)
    reps = LANES // DIFF_DH
    return jnp.tile(cos64, (1, reps)), jnp.tile(sin64, (1, reps))


def _in_proj(x2d, seq, w):
    tokens = x2d.shape[0]
    tm = TOKEN_TILE
    nchunk = tm // GLA_CHUNK
    mf, mb = _chunk_sum_matrices(tm)
    cos, sin = _rope_tables(seq)
    pos_blocks = seq // tm
    row = lambda i: (i, 0)
    pos = lambda i: (i % pos_blocks, 0)

    def tok(width, dtype=BF16):
        return jax.ShapeDtypeStruct((tokens, width), dtype)

    dec = jax.ShapeDtypeStruct((tokens // GLA_CHUNK, GLA_QK), F32)
    return pl.pallas_call(
        _inproj_kernel,
        grid=(tokens // tm,),
        in_specs=[pl.BlockSpec((tm, D_MODEL), row),
                  _const_spec((1, D_MODEL)),
                  _const_spec(w["w_a"].shape),
                  _const_spec(w["w_g"].shape),
                  _const_spec(w["w_up"].shape),
                  _const_spec(w["b_g"].shape),
                  _const_spec(mf.shape),
                  _const_spec(mb.shape),
                  pl.BlockSpec((tm, LANES), pos),
                  pl.BlockSpec((tm, LANES), pos)],
        out_specs=[pl.BlockSpec((tm, 3 * GLA_QK), row),
                   pl.BlockSpec((tm, 3 * GLA_QK), row),
                   pl.BlockSpec((tm, GLA_V), row),
                   pl.BlockSpec((tm, GLA_V), row),
                   pl.BlockSpec((nchunk, GLA_QK), row),
                   pl.BlockSpec((nchunk, GLA_QK), row),
                   pl.BlockSpec((tm, DIFF_QK), row),
                   pl.BlockSpec((tm, DIFF_QK), row),
                   pl.BlockSpec((tm, DIFF_V), row)],
        out_shape=[tok(3 * GLA_QK), tok(3 * GLA_QK), tok(GLA_V), tok(GLA_V), dec, dec,
                   tok(DIFF_QK), tok(DIFF_QK), tok(DIFF_V)],
        compiler_params=_params(("parallel",)),
        name="in_proj",
    )(x2d, w["norm_mix_pre"], w["w_a"], w["w_g"], w["w_up"], w["b_g"], mf, mb, cos, sin)


def _gla_kernel(f_ref, b_ref, vf_ref, vb_ref, decf_ref, decb_ref, of_ref, ob_ref, sf_ref, sb_ref):
    nchunk = f_ref.shape[0] // GLA_CHUNK

    @pl.when(pl.program_id(1) == 0)
    def _():
        sf_ref[...] = jnp.zeros_like(sf_ref)
        sb_ref[...] = jnp.zeros_like(sb_ref)

    qk_head = lax.broadcasted_iota(jnp.int32, (GLA_CHUNK, GLA_QK), 1) // GLA_DK
    st_head = lax.broadcasted_iota(jnp.int32, (GLA_DV, GLA_QK), 1) // GLA_DK
    a_row = lax.broadcasted_iota(jnp.int32, (GLA_HEADS * GLA_CHUNK, GLA_CHUNK), 0) % GLA_CHUNK
    a_col = lax.broadcasted_iota(jnp.int32, (GLA_HEADS * GLA_CHUNK, GLA_CHUNK), 1)

    def chunk_step(x_ref, v_ref, dec_ref, o_ref, s_ref, c, keep):
        r0 = pl.multiple_of(c * GLA_CHUNK, GLA_CHUNK)
        qe = x_ref[pl.ds(r0, GLA_CHUNK), 0:GLA_QK]
        ke = x_ref[pl.ds(r0, GLA_CHUNK), GLA_QK:2 * GLA_QK]
        kd = x_ref[pl.ds(r0, GLA_CHUNK), 2 * GLA_QK:3 * GLA_QK]
        v = v_ref[pl.ds(r0, GLA_CHUNK), :]
        dec = dec_ref[pl.ds(c, 1), :]
        qm = jnp.concatenate([jnp.where(qk_head == h, qe, jnp.zeros_like(qe))
                              for h in range(GLA_HEADS)], axis=0)
        a = jnp.where(keep, _dot_nt(qm, ke), 0.0).astype(BF16)
        st = s_ref[...]
        inter = _dot_nt(qm, st.astype(BF16))
        outs = []
        for h in range(GLA_HEADS):
            rows = slice(h * GLA_CHUNK, (h + 1) * GLA_CHUNK)
            outs.append(_dot(a[rows], v[:, h * GLA_DV:(h + 1) * GLA_DV]) + inter[rows])
        o_ref[pl.ds(r0, GLA_CHUNK), :] = jnp.concatenate(outs, axis=1).astype(BF16)
        ut = _dot_tn(v, kd)
        upd = jnp.zeros_like(st)
        for h in range(GLA_HEADS):
            upd = upd + jnp.where(st_head == h, ut[h * GLA_DV:(h + 1) * GLA_DV], 0.0)
        s_ref[...] = dec * st + upd

    def body(c, carry):
        chunk_step(f_ref, vf_ref, decf_ref, of_ref, sf_ref, c, a_col <= a_row)
        chunk_step(b_ref, vb_ref, decb_ref, ob_ref, sb_ref, nchunk - 1 - c, a_col >= a_row)
        return carry

    lax.fori_loop(0, nchunk, body, 0)


def _gla(f, b, v, decf, decb, batch, seq):
    tokens = f.shape[0]
    lb = GLA_BLOCK
    nblk = seq // lb
    nchunk = lb // GLA_CHUNK
    fwd = lambda bi, i: (bi * nblk + i, 0)
    bwd = lambda bi, i: (bi * nblk + nblk - 1 - i, 0)
    out = jax.ShapeDtypeStruct((tokens, GLA_V), BF16)
    return pl.pallas_call(
        _gla_kernel,
        grid=(batch, nblk),
        in_specs=[pl.BlockSpec((lb, 3 * GLA_QK), fwd),
                  pl.BlockSpec((lb, 3 * GLA_QK), bwd),
                  pl.BlockSpec((lb, GLA_V), fwd),
                  pl.BlockSpec((lb, GLA_V), bwd),
                  pl.BlockSpec((nchunk, GLA_QK), fwd),
                  pl.BlockSpec((nchunk, GLA_QK), bwd)],
        out_specs=[pl.BlockSpec((lb, GLA_V), fwd),
                   pl.BlockSpec((lb, GLA_V), bwd)],
        out_shape=[out, out],
        scratch_shapes=[pltpu.VMEM((GLA_DV, GLA_QK), F32),
                        pltpu.VMEM((GLA_DV, GLA_QK), F32)],
        compiler_params=_params(("parallel", "arbitrary")),
        name="gla",
    )(f, b, v, v, decf, decb)


def _diff_kernel(q_ref, k_ref, v_ref, lq1_ref, lk1_ref, lq2_ref, lk2_ref, w_ref, o_ref,
                 qm_ref, m_ref, l_ref, acc_ref, *, lam_init):
    ki = pl.program_id(2)
    tq = q_ref.shape[0]
    tk = k_ref.shape[0]
    nstream = 2 * DIFF_HEADS
    width = 2 * DIFF_DH

    @pl.when(ki == 0)
    def _():
        m_ref[...] = jnp.full_like(m_ref, -jnp.inf)
        l_ref[...] = jnp.zeros_like(l_ref)
        acc_ref[...] = jnp.zeros_like(acc_ref)
        first = lax.broadcasted_iota(jnp.int32, (tq, width), 1) < DIFF_DH
        for h in range(DIFF_HEADS):
            qh = q_ref[:, h * width:(h + 1) * width]
            zero = jnp.zeros_like(qh)
            qm_ref[2 * h] = jnp.where(first, qh, zero)
            qm_ref[2 * h + 1] = jnp.where(first, zero, qh)

    for h in range(DIFF_HEADS):
        kh = k_ref[:, h * width:(h + 1) * width]
        vh = v_ref[:, h * width:(h + 1) * width]
        for c in range(2):
            s_idx = 2 * h + c
            s = _dot_nt(qm_ref[s_idx], kh)
            m_prev = m_ref[s_idx]
            m_next = jnp.maximum(m_prev, jnp.max(s, axis=1, keepdims=True))
            p = jnp.exp(s - jnp.tile(m_next, (1, tk // LANES)))
            alpha = jnp.exp(m_prev - m_next)
            l_ref[s_idx] = alpha * l_ref[s_idx] + jnp.sum(p, axis=1, keepdims=True)
            acc_ref[s_idx] = alpha * acc_ref[s_idx] + _dot(p.astype(BF16), vh)
            m_ref[s_idx] = m_next

    @pl.when(ki == pl.num_programs(2) - 1)
    def _():
        lam = (jnp.exp(jnp.sum(lq1_ref[...] * lk1_ref[...], axis=1, keepdims=True))
               - jnp.exp(jnp.sum(lq2_ref[...] * lk2_ref[...], axis=1, keepdims=True)) + lam_init)
        for h in range(DIFF_HEADS):
            o1 = acc_ref[2 * h] / l_ref[2 * h]
            o2 = acc_ref[2 * h + 1] / l_ref[2 * h + 1]
            o = _rms(o1 - lam * o2, w_ref[...]) * (1.0 - lam_init)
            o_ref[:, h * width:(h + 1) * width] = o.astype(BF16)


def _diff_attn(dq, dk, dv, w, batch, seq, lam_init):
    tokens = dq.shape[0]
    tq = min(ATTN_Q_TILE, seq)
    tk = min(ATTN_K_TILE, seq)
    nq, nk = seq // tq, seq // tk
    qmap = lambda b, qi, ki: (b * nq + qi, 0)
    kmap = lambda b, qi, ki: (b * nk + ki, 0)
    nstream = 2 * DIFF_HEADS
    vec = _const_spec((1, DIFF_DH))
    return pl.pallas_call(
        functools.partial(_diff_kernel, lam_init=lam_init),
        grid=(batch, nq, nk),
        in_specs=[pl.BlockSpec((tq, DIFF_QK), qmap),
                  pl.BlockSpec((tk, DIFF_QK), kmap),
                  pl.BlockSpec((tk, DIFF_V), kmap),
                  vec, vec, vec, vec,
                  _const_spec((1, 2 * DIFF_DH))],
        out_specs=pl.BlockSpec((tq, DIFF_V), qmap),
        out_shape=jax.ShapeDtypeStruct((tokens, DIFF_V), BF16),
        scratch_shapes=[pltpu.VMEM((nstream, tq, 2 * DIFF_DH), BF16),
                        pltpu.VMEM((nstream, tq, LANES), F32),
                        pltpu.VMEM((nstream, tq, LANES), F32),
                        pltpu.VMEM((nstream, tq, 2 * DIFF_DH), F32)],
        compiler_params=_params(("parallel", "parallel", "arbitrary")),
        name="diff_attn",
    )(dq, dk, dv, w["lambda_q1"], w["lambda_k1"], w["lambda_q2"], w["lambda_k2"], w["diff_subln_w"])


def _mix_xattn_kernel(x_ref, of_ref, ob_ref, og_ref, od_ref, k_ref, v_ref,
                      gnw_ref, wout_ref, npost_ref, nxpre_ref, wxq_ref, wxo_ref, nxpost_ref, y_ref):
    o = of_ref[...].astype(F32) + ob_ref[...].astype(F32)
    og = og_ref[...].astype(F32)
    gate = og * (1.0 / (1.0 + jnp.exp(-og)))
    heads = []
    for h in range(GLA_HEADS):
        cols = slice(h * GLA_DV, (h + 1) * GLA_DV)
        heads.append((_rms(o[:, cols], gnw_ref[...]) * gate[:, cols]).astype(BF16))
    mix_in = jnp.concatenate(heads + [od_ref[...]], axis=1)
    mix = _dot(mix_in, wout_ref[...])
    x1 = x_ref[...] + _rms(mix, npost_ref[...])

    h2 = _rms(x1, nxpre_ref[...]).astype(BF16)
    q = (_dot(h2, wxq_ref[...]) * (XATTN_DH ** -0.5)).astype(BF16)
    outs = []
    for h in range(XATTN_HEADS):
        cols = slice(h * XATTN_DH, (h + 1) * XATTN_DH)
        s = _dot_nt(q[:, cols], k_ref[:, cols])
        p = jnp.exp(s - jnp.max(s, axis=1, keepdims=True))
        denom = jnp.sum(p, axis=1, keepdims=True)
        outs.append((_dot(p.astype(BF16), v_ref[:, cols]) / denom).astype(BF16))
    xo = _dot(jnp.concatenate(outs, axis=1), wxo_ref[...])
    y_ref[...] = x1 + _rms(xo, nxpost_ref[...])


def _mix_xattn(x2d, o_f, o_b, og, o_diff, mem_k, mem_v, seq, w):
    tokens = x2d.shape[0]
    tm = TOKEN_TILE
    per_seq = seq // tm
    row = lambda i: (i, 0)
    memmap = lambda i: (i // per_seq, 0)
    half = pl.BlockSpec((tm, GLA_V), row)
    sq = _const_spec((D_MODEL, D_MODEL))
    nv = _const_spec((1, D_MODEL))
    return pl.pallas_call(
        _mix_xattn_kernel,
        grid=(tokens // tm,),
        in_specs=[pl.BlockSpec((tm, D_MODEL), row), half, half, half, half,
                  pl.BlockSpec((MEM_LEN, D_MODEL), memmap),
                  pl.BlockSpec((MEM_LEN, D_MODEL), memmap),
                  _const_spec((1, GLA_DV)), sq, nv, nv, sq, sq, nv],
        out_specs=pl.BlockSpec((tm, D_MODEL), row),
        out_shape=jax.ShapeDtypeStruct((tokens, D_MODEL), F32),
        compiler_params=_params(("parallel",)),
        name="mix_xattn",
    )(x2d, o_f, o_b, og, o_diff, mem_k, mem_v, w["gla_norm_w"], w["w_out"], w["norm_mix_post"],
      w["norm_xattn_pre"], w["w_xq"], w["w_xo"], w["norm_xattn_post"])


def _ffn_kernel(x_ref, npre_ref, wg_ref, wu_ref, wd_ref, npost_ref, y_ref):
    x = x_ref[...]
    h = _rms(x, npre_ref[...]).astype(BF16)
    g = _dot(h, wg_ref[...])
    u = _dot(h, wu_ref[...])
    a = (g * (1.0 / (1.0 + jnp.exp(-g))) * u).astype(BF16)
    y_ref[...] = x + _rms(_dot(a, wd_ref[...]), npost_ref[...])


def _ffn(x2d, w):
    tokens = x2d.shape[0]
    tm = TOKEN_TILE
    row = lambda i: (i, 0)
    nv = _const_spec((1, D_MODEL))
    return pl.pallas_call(
        _ffn_kernel,
        grid=(tokens // tm,),
        in_specs=[pl.BlockSpec((tm, D_MODEL), row), nv,
                  _const_spec((D_MODEL, D_FF)), _const_spec((D_MODEL, D_FF)),
                  _const_spec((D_FF, D_MODEL)), nv],
        out_specs=pl.BlockSpec((tm, D_MODEL), row),
        out_shape=jax.ShapeDtypeStruct((tokens, D_MODEL), F32),
        compiler_params=_params(("parallel",)),
        name="ffn",
    )(x2d, w["norm_ffn_pre"], w["w_ffn_gate"], w["w_ffn_up"], w["w_ffn_down"], w["norm_ffn_post"])


def _prepare_weights(p):
    w_in = p["w_in"]
    gates_at = 2 * GLA_QK + GLA_V
    gates_end = gates_at + 2 * GLA_GATE_RANK
    w = {}
    w["w_a"] = jnp.concatenate([w_in[:, :gates_at], w_in[:, gates_end:]], axis=1).astype(BF16)
    w["w_g"] = jnp.pad(w_in[:, gates_at:gates_end], ((0, 0), (0, LANES - 2 * GLA_GATE_RANK))).astype(BF16)
    w_up = jnp.zeros((LANES, 2 * GLA_QK), F32)
    w_up = w_up.at[:GLA_GATE_RANK, :GLA_QK].set(p["w_gate_up_f"])
    w_up = w_up.at[GLA_GATE_RANK:2 * GLA_GATE_RANK, GLA_QK:].set(p["w_gate_up_b"])
    w["w_up"] = w_up.astype(BF16)
    w["b_g"] = jnp.concatenate([p["b_gate_f"], p["b_gate_b"]])[None, :]
    for name in ("w_out", "w_xq", "w_xkv", "w_xo", "w_ffn_gate", "w_ffn_up", "w_ffn_down"):
        w[name] = p[name].astype(BF16)
    for name in ("norm_mix_pre", "gla_norm_w", "lambda_q1", "lambda_k1", "lambda_q2", "lambda_k2",
                 "diff_subln_w", "norm_mix_post", "norm_xattn_pre", "norm_mem", "norm_xattn_post",
                 "norm_ffn_pre", "norm_ffn_post"):
        w[name] = p[name][None, :]
    return w


def _encoder_layer(x, mem, w, lam_init):
    batch, seq, _ = x.shape
    assert seq % TOKEN_TILE == 0 and seq % GLA_BLOCK == 0 and mem.shape[1] == MEM_LEN
    x2d = x.reshape(batch * seq, D_MODEL)
    mem_k, mem_v = _mem_kv(mem.reshape(batch * MEM_LEN, D_MODEL), w["norm_mem"], w["w_xkv"])
    f, b, v, og, decf, decb, dq, dk, dv = _in_proj(x2d, seq, w)
    o_f, o_b = _gla(f, b, v, decf, decb, batch, seq)
    o_diff = _diff_attn(dq, dk, dv, w, batch, seq, lam_init)
    x2 = _mix_xattn(x2d, o_f, o_b, og, o_diff, mem_k, mem_v, seq, w)
    return _ffn(x2, w).reshape(batch, seq, D_MODEL)


def kernel(x_prompt, x_sample, mem_prompt, mem_sample, norm_mix_pre, w_in, w_gate_up_f, b_gate_f,
           w_gate_up_b, b_gate_b, gla_norm_w, lambda_q1, lambda_k1, lambda_q2, lambda_k2, diff_subln_w,
           w_out, norm_mix_post, norm_xattn_pre, norm_mem, w_xq, w_xkv, w_xo, norm_xattn_post,
           norm_ffn_pre, w_ffn_gate, w_ffn_up, w_ffn_down, norm_ffn_post):
    stacked = dict(
        norm_mix_pre=norm_mix_pre, w_in=w_in, w_gate_up_f=w_gate_up_f, b_gate_f=b_gate_f,
        w_gate_up_b=w_gate_up_b, b_gate_b=b_gate_b, gla_norm_w=gla_norm_w, lambda_q1=lambda_q1,
        lambda_k1=lambda_k1, lambda_q2=lambda_q2, lambda_k2=lambda_k2, diff_subln_w=diff_subln_w,
        w_out=w_out, norm_mix_post=norm_mix_post, norm_xattn_pre=norm_xattn_pre, norm_mem=norm_mem,
        w_xq=w_xq, w_xkv=w_xkv, w_xo=w_xo, norm_xattn_post=norm_xattn_post,
        norm_ffn_pre=norm_ffn_pre, w_ffn_gate=w_ffn_gate, w_ffn_up=w_ffn_up, w_ffn_down=w_ffn_down,
        norm_ffn_post=norm_ffn_post)
    y_prompt, y_sample = x_prompt, x_sample
    for layer in range(w_in.shape[0]):
        lam_init = 0.8 - 0.6 * math.exp(-0.3 * layer)
        w = _prepare_weights({name: t[layer] for name, t in stacked.items()})
        y_prompt = _encoder_layer(y_prompt, mem_prompt, w, lam_init)
        y_sample = _encoder_layer(y_sample, mem_sample, w, lam_init)
    return (y_prompt, y_sample)
```

```python
import functools
import math

import numpy as np
import jax
import jax.numpy as jnp
from jax import lax
from jax.experimental import pallas as pl
from jax.experimental.pallas import tpu as pltpu

F32 = jnp.float32
BF16 = jnp.bfloat16

D_MODEL = 1024
GLA_HEADS = 4
GLA_DK = 64
GLA_DV = 128
GLA_QK = GLA_HEADS * GLA_DK
GLA_V = GLA_HEADS * GLA_DV
GLA_GATE_RANK = 16
GLA_GATE_NORMALIZER = 16.0
GLA_CHUNK = 64
DIFF_HEADS = 4
DIFF_DH = 64
DIFF_QK = DIFF_HEADS * 2 * DIFF_DH
DIFF_V = DIFF_HEADS * 2 * DIFF_DH
ROT_DIM = DIFF_DH // 4
ROPE_THETA = 500000.0
MEM_LEN = 256
XATTN_HEADS = 4
XATTN_DH = D_MODEL // XATTN_HEADS
D_FF = 2816
RMS_EPS = 1e-6

LANES = 128
TOKEN_TILE = 512
GLA_BLOCK = 512
ATTN_Q_TILE = 512
ATTN_K_TILE = 2048
ATTN_ROW_BLOCK = 16
LOG2E = math.log2(math.e)
VMEM_LIMIT = 56 * 1024 * 1024

NT_DIMS = (((1,), (1,)), ((), ()))
TN_DIMS = (((0,), (0,)), ((), ()))


def _rms(x, w):
    return x * lax.rsqrt(jnp.mean(x * x, axis=-1, keepdims=True) + RMS_EPS) * w


def _dot(a, b):
    return jnp.dot(a, b, preferred_element_type=F32)


def _dot_nt(a, b):
    return lax.dot_general(a, b, NT_DIMS, preferred_element_type=F32)


def _dot_tn(a, b):
    return lax.dot_general(a, b, TN_DIMS, preferred_element_type=F32)


def _const_spec(shape):
    return pl.BlockSpec(shape, lambda *_: (0,) * len(shape))


def _params(semantics):
    return pltpu.CompilerParams(dimension_semantics=semantics, vmem_limit_bytes=VMEM_LIMIT)


def _mem_kv_kernel(m_ref, nw_ref, w_ref, k_ref, v_ref):
    m = _rms(m_ref[...], nw_ref[...]).astype(BF16)
    kv = _dot(m, w_ref[...])
    k_ref[...] = kv[:, :D_MODEL].astype(BF16)
    v_ref[...] = kv[:, D_MODEL:].astype(BF16)


def _mem_kv(mem2d, norm_w, w_xkv):
    rows = mem2d.shape[0]
    out = jax.ShapeDtypeStruct((rows, D_MODEL), BF16)
    return pl.pallas_call(
        _mem_kv_kernel,
        grid=(rows // MEM_LEN,),
        in_specs=[pl.BlockSpec((MEM_LEN, D_MODEL), lambda i: (i, 0)),
                  _const_spec((1, D_MODEL)),
                  _const_spec((D_MODEL, 2 * D_MODEL))],
        out_specs=[pl.BlockSpec((MEM_LEN, D_MODEL), lambda i: (i, 0))] * 2,
        out_shape=[out, out],
        compiler_params=_params(("parallel",)),
        name="mem_kv",
    )(mem2d, norm_w, w_xkv)


def _inproj_kernel(x_ref, nw_ref, wa_ref, wg_ref, wup_ref, bg_ref, mf_ref, mb_ref, cos_ref, sin_ref,
                   f_ref, b_ref, v_ref, og_ref, decf_ref, decb_ref, dq_ref, dk_ref, dv_ref):
    tm = x_ref.shape[0]
    h = _rms(x_ref[...], nw_ref[...]).astype(BF16)

    gdown = _dot(h, wg_ref[...]).astype(BF16)
    pre = _dot(gdown, wup_ref[...]) + bg_ref[...]
    g = (jnp.minimum(pre, 0.0) - jnp.log1p(jnp.exp(-jnp.abs(pre)))) * (1.0 / GLA_GATE_NORMALIZER)
    g_hi = g.astype(BF16)
    g_lo = (g - g_hi.astype(F32)).astype(BF16)

    def chunk_sums(m_ref, lo, hi):
        m = m_ref[...]
        return _dot(m, g_hi[:, lo:hi]) + _dot(m, g_lo[:, lo:hi])

    rf = chunk_sums(mf_ref, 0, GLA_QK)
    rb = chunk_sums(mb_ref, GLA_QK, 2 * GLA_QK)

    qk = _dot(h, wa_ref[:, 0:2 * GLA_QK])
    q = qk[:, :GLA_QK] * (GLA_DK ** -0.5)
    k = qk[:, GLA_QK:]
    for r, o_ref, dec_ref in ((rf, f_ref, decf_ref), (rb, b_ref, decb_ref)):
        cum = r[0:tm]
        o_ref[:, 0:GLA_QK] = (q * jnp.exp(cum)).astype(BF16)
        o_ref[:, GLA_QK:2 * GLA_QK] = (k * jnp.exp(-cum)).astype(BF16)
        o_ref[:, 2 * GLA_QK:3 * GLA_QK] = (k * jnp.exp(r[tm:2 * tm])).astype(BF16)
        dec_ref[...] = jnp.exp(r[2 * tm:])

    off = 2 * GLA_QK
    v_ref[...] = _dot(h, wa_ref[:, off:off + GLA_V]).astype(BF16)
    off += GLA_V
    og_ref[...] = _dot(h, wa_ref[:, off:off + GLA_V]).astype(BF16)
    off += GLA_V

    cos = cos_ref[...]
    sin = sin_ref[...]
    lane = lax.broadcasted_iota(jnp.int32, (tm, LANES), 1)
    second_half = (lane & (DIFF_DH - 1)) >= (ROT_DIM // 2)

    def rope(t):
        slabs = []
        for s in range(t.shape[1] // LANES):
            ts = t[:, s * LANES:(s + 1) * LANES]
            partner = jnp.where(second_half, pltpu.roll(ts, ROT_DIM // 2, 1),
                                pltpu.roll(ts, LANES - ROT_DIM // 2, 1))
            slabs.append(ts * cos + partner * sin)
        return jnp.concatenate(slabs, axis=1)

    dq = _dot(h, wa_ref[:, off:off + DIFF_QK])
    dq_ref[...] = (rope(dq) * (DIFF_DH ** -0.5 * LOG2E)).astype(BF16)
    off += DIFF_QK
    dk = _dot(h, wa_ref[:, off:off + DIFF_QK])
    dk_ref[...] = rope(dk).astype(BF16)
    off += DIFF_QK
    dv_ref[...] = _dot(h, wa_ref[:, off:off + DIFF_V]).astype(BF16)


def _chunk_sum_matrices(tm):
    r = np.arange(tm)[:, None]
    c = np.arange(tm)[None, :]
    same = (r // GLA_CHUNK) == (c // GLA_CHUNK)
    total = (np.arange(tm // GLA_CHUNK)[:, None] == (c // GLA_CHUNK))
    fwd = np.concatenate([same & (c <= r), same & (c > r), total], axis=0)
    bwd = np.concatenate([same & (c >= r), same & (c < r), total], axis=0)
    return jnp.asarray(fwd, BF16), jnp.asarray(bwd, BF16)


def _rope_tables(seq):
    inv = ROPE_THETA ** (-jnp.arange(0, ROT_DIM, 2, dtype=F32) / ROT_DIM)
    ang = jnp.arange(seq).astype(F32)[:, None] * inv[None, :]
    cos, sin = jnp.cos(ang), jnp.sin(ang)
    rest = DIFF_DH - ROT_DIM
    cos64 = jnp.concatenate([cos, cos, jnp.ones((seq, rest), F32)], axis=1)
    sin64 = jnp.concatenate([-sin, sin, jnp.zeros((seq, rest), F32)], axis=1)
    reps = LANES // DIFF_DH
    return jnp.tile(cos64, (1, reps)), jnp.tile(sin64, (1, reps))


def _in_proj(x2d, seq, w):
    tokens = x2d.shape[0]
    tm = TOKEN_TILE
    nchunk = tm // GLA_CHUNK
    mf, mb = _chunk_sum_matrices(tm)
    cos, sin = _rope_tables(seq)
    pos_blocks = seq // tm
    row = lambda i: (i, 0)
    pos = lambda i: (i % pos_blocks, 0)

    def tok(width, dtype=BF16):
        return jax.ShapeDtypeStruct((tokens, width), dtype)

    dec = jax.ShapeDtypeStruct((tokens // GLA_CHUNK, GLA_QK), F32)
    return pl.pallas_call(
        _inproj_kernel,
        grid=(tokens // tm,),
        in_specs=[pl.BlockSpec((tm, D_MODEL), row),
                  _const_spec((1, D_MODEL)),
                  _const_spec(w["w_a"].shape),
                  _const_spec(w["w_g"].shape),
                  _const_spec(w["w_up"].shape),
                  _const_spec(w["b_g"].shape),
                  _const_spec(mf.shape),
                  _const_spec(mb.shape),
                  pl.BlockSpec((tm, LANES), pos),
                  pl.BlockSpec((tm, LANES), pos)],
        out_specs=[pl.BlockSpec((tm, 3 * GLA_QK), row),
                   pl.BlockSpec((tm, 3 * GLA_QK), row),
                   pl.BlockSpec((tm, GLA_V), row),
                   pl.BlockSpec((tm, GLA_V), row),
                   pl.BlockSpec((nchunk, GLA_QK), row),
                   pl.BlockSpec((nchunk, GLA_QK), row),
                   pl.BlockSpec((tm, DIFF_QK), row),
                   pl.BlockSpec((tm, DIFF_QK), row),
                   pl.BlockSpec((tm, DIFF_V), row)],
        out_shape=[tok(3 * GLA_QK), tok(3 * GLA_QK), tok(GLA_V), tok(GLA_V), dec, dec,
                   tok(DIFF_QK), tok(DIFF_QK), tok(DIFF_V)],
        compiler_params=_params(("parallel",)),
        name="in_proj",
    )(x2d, w["norm_mix_pre"], w["w_a"], w["w_g"], w["w_up"], w["b_g"], mf, mb, cos, sin)


def _gla_kernel(f_ref, b_ref, vf_ref, vb_ref, decf_ref, decb_ref, of_ref, ob_ref, sf_ref, sb_ref):
    nchunk = f_ref.shape[0] // GLA_CHUNK

    @pl.when(pl.program_id(1) == 0)
    def _():
        sf_ref[...] = jnp.zeros_like(sf_ref)
        sb_ref[...] = jnp.zeros_like(sb_ref)

    qk_head = lax.broadcasted_iota(jnp.int32, (GLA_CHUNK, GLA_QK), 1) // GLA_DK
    st_head = lax.broadcasted_iota(jnp.int32, (GLA_DV, GLA_QK), 1) // GLA_DK
    a_row = lax.broadcasted_iota(jnp.int32, (GLA_HEADS * GLA_CHUNK, GLA_CHUNK), 0) % GLA_CHUNK
    a_col = lax.broadcasted_iota(jnp.int32, (GLA_HEADS * GLA_CHUNK, GLA_CHUNK), 1)

    def chunk_step(x_ref, v_ref, dec_ref, o_ref, s_ref, c, keep):
        r0 = pl.multiple_of(c * GLA_CHUNK, GLA_CHUNK)
        qe = x_ref[pl.ds(r0, GLA_CHUNK), 0:GLA_QK]
        ke = x_ref[pl.ds(r0, GLA_CHUNK), GLA_QK:2 * GLA_QK]
        kd = x_ref[pl.ds(r0, GLA_CHUNK), 2 * GLA_QK:3 * GLA_QK]
        v = v_ref[pl.ds(r0, GLA_CHUNK), :]
        dec = dec_ref[pl.ds(c, 1), :]
        qm = jnp.concatenate([jnp.where(qk_head == h, qe, jnp.zeros_like(qe))
                              for h in range(GLA_HEADS)], axis=0)
        a = jnp.where(keep, _dot_nt(qm, ke), 0.0).astype(BF16)
        st = s_ref[...]
        inter = _dot_nt(qm, st.astype(BF16))
        outs = []
        for h in range(GLA_HEADS):
            rows = slice(h * GLA_CHUNK, (h + 1) * GLA_CHUNK)
            outs.append(_dot(a[rows], v[:, h * GLA_DV:(h + 1) * GLA_DV]) + inter[rows])
        o_ref[pl.ds(r0, GLA_CHUNK), :] = jnp.concatenate(outs, axis=1).astype(BF16)
        ut = _dot_tn(v, kd)
        upd = jnp.zeros_like(st)
        for h in range(GLA_HEADS):
            upd = upd + jnp.where(st_head == h, ut[h * GLA_DV:(h + 1) * GLA_DV], 0.0)
        s_ref[...] = dec * st + upd

    def body(c, carry):
        chunk_step(f_ref, vf_ref, decf_ref, of_ref, sf_ref, c, a_col <= a_row)
        chunk_step(b_ref, vb_ref, decb_ref, ob_ref, sb_ref, nchunk - 1 - c, a_col >= a_row)
        return carry

    lax.fori_loop(0, nchunk, body, 0)


def _gla(f, b, v, decf, decb, batch, seq):
    tokens = f.shape[0]
    lb = GLA_BLOCK
    nblk = seq // lb
    nchunk = lb // GLA_CHUNK
    fwd = lambda bi, i: (bi * nblk + i, 0)
    bwd = lambda bi, i: (bi * nblk + nblk - 1 - i, 0)
    out = jax.ShapeDtypeStruct((tokens, GLA_V), BF16)
    return pl.pallas_call(
        _gla_kernel,
        grid=(batch, nblk),
        in_specs=[pl.BlockSpec((lb, 3 * GLA_QK), fwd),
                  pl.BlockSpec((lb, 3 * GLA_QK), bwd),
                  pl.BlockSpec((lb, GLA_V), fwd),
                  pl.BlockSpec((lb, GLA_V), bwd),
                  pl.BlockSpec((nchunk, GLA_QK), fwd),
                  pl.BlockSpec((nchunk, GLA_QK), bwd)],
        out_specs=[pl.BlockSpec((lb, GLA_V), fwd),
                   pl.BlockSpec((lb, GLA_V), bwd)],
        out_shape=[out, out],
        scratch_shapes=[pltpu.VMEM((GLA_DV, GLA_QK), F32),
                        pltpu.VMEM((GLA_DV, GLA_QK), F32)],
        compiler_params=_params(("parallel", "arbitrary")),
        name="gla",
    )(f, b, v, v, decf, decb)


def _diff_kernel(q_ref, k_ref, v_ref, lq1_ref, lk1_ref, lq2_ref, lk2_ref, w_ref, o_ref,
                 qm_ref, m_ref, l_ref, acc_ref, s_ref, p_ref, *, lam_init):
    ki = pl.program_id(2)
    tq = q_ref.shape[0]
    tk = k_ref.shape[0]
    width = 2 * DIFF_DH

    @pl.when(ki == 0)
    def _():
        m_ref[...] = jnp.full_like(m_ref, -jnp.inf)
        l_ref[...] = jnp.zeros_like(l_ref)
        acc_ref[...] = jnp.zeros_like(acc_ref)
        first = lax.broadcasted_iota(jnp.int32, (tq, width), 1) < DIFF_DH
        for h in range(DIFF_HEADS):
            qh = q_ref[:, h * width:(h + 1) * width]
            zero = jnp.zeros_like(qh)
            qm_ref[2 * h] = jnp.where(first, qh, zero)
            qm_ref[2 * h + 1] = jnp.where(first, zero, qh)

    for h in range(DIFF_HEADS):
        kh = k_ref[:, h * width:(h + 1) * width]
        vh = v_ref[:, h * width:(h + 1) * width]
        for c in range(2):
            s_idx = 2 * h + c
            buf = s_idx % 2
            s = _dot_nt(qm_ref[s_idx], kh)
            s_ref[buf] = s
            m_prev = m_ref[s_idx]
            m_next = jnp.maximum(m_prev, jnp.max(s, axis=1, keepdims=True))
            m_ref[s_idx] = m_next
            sums = []
            for r0 in range(0, tq, ATTN_ROW_BLOCK):
                rows = slice(r0, r0 + ATTN_ROW_BLOCK)
                pb = jnp.exp2(s_ref[buf, rows, :] - jnp.tile(m_next[rows], (1, tk // LANES)))
                sums.append(jnp.sum(pb, axis=1, keepdims=True))
                p_ref[buf, rows, :] = pb.astype(BF16)
            alpha = jnp.exp2(m_prev - m_next)
            l_ref[s_idx] = alpha * l_ref[s_idx] + jnp.concatenate(sums, axis=0)
            acc_ref[s_idx] = alpha * acc_ref[s_idx] + _dot(p_ref[buf], vh)

    @pl.when(ki == pl.num_programs(2) - 1)
    def _():
        lam = (jnp.exp(jnp.sum(lq1_ref[...] * lk1_ref[...], axis=1, keepdims=True))
               - jnp.exp(jnp.sum(lq2_ref[...] * lk2_ref[...], axis=1, keepdims=True)) + lam_init)
        for h in range(DIFF_HEADS):
            o1 = acc_ref[2 * h] / l_ref[2 * h]
            o2 = acc_ref[2 * h + 1] / l_ref[2 * h + 1]
            o = _rms(o1 - lam * o2, w_ref[...]) * (1.0 - lam_init)
            o_ref[:, h * width:(h + 1) * width] = o.astype(BF16)


def _diff_attn(dq, dk, dv, w, batch, seq, lam_init):
    tokens = dq.shape[0]
    tq = min(ATTN_Q_TILE, seq)
    tk = min(ATTN_K_TILE, seq)
    nq, nk = seq // tq, seq // tk
    qmap = lambda b, qi, ki: (b * nq + qi, 0)
    kmap = lambda b, qi, ki: (b * nk + ki, 0)
    nstream = 2 * DIFF_HEADS
    vec = _const_spec((1, DIFF_DH))
    return pl.pallas_call(
        functools.partial(_diff_kernel, lam_init=lam_init),
        grid=(batch, nq, nk),
        in_specs=[pl.BlockSpec((tq, DIFF_QK), qmap),
                  pl.BlockSpec((tk, DIFF_QK), kmap),
                  pl.BlockSpec((tk, DIFF_V), kmap),
                  vec, vec, vec, vec,
                  _const_spec((1, 2 * DIFF_DH))],
        out_specs=pl.BlockSpec((tq, DIFF_V), qmap),
        out_shape=jax.ShapeDtypeStruct((tokens, DIFF_V), BF16),
        scratch_shapes=[pltpu.VMEM((nstream, tq, 2 * DIFF_DH), BF16),
                        pltpu.VMEM((nstream, tq, LANES), F32),
                        pltpu.VMEM((nstream, tq, LANES), F32),
                        pltpu.VMEM((nstream, tq, 2 * DIFF_DH), F32),
                        pltpu.VMEM((2, tq, tk), F32),
                        pltpu.VMEM((2, tq, tk), BF16)],
        compiler_params=_params(("parallel", "parallel", "arbitrary")),
        name="diff_attn",
    )(dq, dk, dv, w["lambda_q1"], w["lambda_k1"], w["lambda_q2"], w["lambda_k2"], w["diff_subln_w"])


def _mix_xattn_kernel(x_ref, of_ref, ob_ref, og_ref, od_ref, k_ref, v_ref,
                      gnw_ref, wout_ref, npost_ref, nxpre_ref, wxq_ref, wxo_ref, nxpost_ref, y_ref):
    o = of_ref[...].astype(F32) + ob_ref[...].astype(F32)
    og = og_ref[...].astype(F32)
    gate = og * (1.0 / (1.0 + jnp.exp(-og)))
    heads = []
    for h in range(GLA_HEADS):
        cols = slice(h * GLA_DV, (h + 1) * GLA_DV)
        heads.append((_rms(o[:, cols], gnw_ref[...]) * gate[:, cols]).astype(BF16))
    mix_in = jnp.concatenate(heads + [od_ref[...]], axis=1)
    mix = _dot(mix_in, wout_ref[...])
    x1 = x_ref[...] + _rms(mix, npost_ref[...])

    h2 = _rms(x1, nxpre_ref[...]).astype(BF16)
    q = (_dot(h2, wxq_ref[...]) * (XATTN_DH ** -0.5)).astype(BF16)
    outs = []
    for h in range(XATTN_HEADS):
        cols = slice(h * XATTN_DH, (h + 1) * XATTN_DH)
        s = _dot_nt(q[:, cols], k_ref[:, cols])
        p = jnp.exp(s - jnp.max(s, axis=1, keepdims=True))
        denom = jnp.sum(p, axis=1, keepdims=True)
        outs.append((_dot(p.astype(BF16), v_ref[:, cols]) / denom).astype(BF16))
    xo = _dot(jnp.concatenate(outs, axis=1), wxo_ref[...])
    y_ref[...] = x1 + _rms(xo, nxpost_ref[...])


def _mix_xattn(x2d, o_f, o_b, og, o_diff, mem_k, mem_v, seq, w):
    tokens = x2d.shape[0]
    tm = TOKEN_TILE
    per_seq = seq // tm
    row = lambda i: (i, 0)
    memmap = lambda i: (i // per_seq, 0)
    half = pl.BlockSpec((tm, GLA_V), row)
    sq = _const_spec((D_MODEL, D_MODEL))
    nv = _const_spec((1, D_MODEL))
    return pl.pallas_call(
        _mix_xattn_kernel,
        grid=(tokens // tm,),
        in_specs=[pl.BlockSpec((tm, D_MODEL), row), half, half, half, half,
                  pl.BlockSpec((MEM_LEN, D_MODEL), memmap),
                  pl.BlockSpec((MEM_LEN, D_MODEL), memmap),
                  _const_spec((1, GLA_DV)), sq, nv, nv, sq, sq, nv],
        out_specs=pl.BlockSpec((tm, D_MODEL), row),
        out_shape=jax.ShapeDtypeStruct((tokens, D_MODEL), F32),
        compiler_params=_params(("parallel",)),
        name="mix_xattn",
    )(x2d, o_f, o_b, og, o_diff, mem_k, mem_v, w["gla_norm_w"], w["w_out"], w["norm_mix_post"],
      w["norm_xattn_pre"], w["w_xq"], w["w_xo"], w["norm_xattn_post"])


def _ffn_kernel(x_ref, npre_ref, wg_ref, wu_ref, wd_ref, npost_ref, y_ref):
    x = x_ref[...]
    h = _rms(x, npre_ref[...]).astype(BF16)
    g = _dot(h, wg_ref[...])
    u = _dot(h, wu_ref[...])
    a = (g * (1.0 / (1.0 + jnp.exp(-g))) * u).astype(BF16)
    y_ref[...] = x + _rms(_dot(a, wd_ref[...]), npost_ref[...])


def _ffn(x2d, w):
    tokens = x2d.shape[0]
    tm = TOKEN_TILE
    row = lambda i: (i, 0)
    nv = _const_spec((1, D_MODEL))
    return pl.pallas_call(
        _ffn_kernel,
        grid=(tokens // tm,),
        in_specs=[pl.BlockSpec((tm, D_MODEL), row), nv,
                  _const_spec((D_MODEL, D_FF)), _const_spec((D_MODEL, D_FF)),
                  _const_spec((D_FF, D_MODEL)), nv],
        out_specs=pl.BlockSpec((tm, D_MODEL), row),
        out_shape=jax.ShapeDtypeStruct((tokens, D_MODEL), F32),
        compiler_params=_params(("parallel",)),
        name="ffn",
    )(x2d, w["norm_ffn_pre"], w["w_ffn_gate"], w["w_ffn_up"], w["w_ffn_down"], w["norm_ffn_post"])


def _prepare_weights(p):
    w_in = p["w_in"]
    gates_at = 2 * GLA_QK + GLA_V
    gates_end = gates_at + 2 * GLA_GATE_RANK
    w = {}
    w["w_a"] = jnp.concatenate([w_in[:, :gates_at], w_in[:, gates_end:]], axis=1).astype(BF16)
    w["w_g"] = jnp.pad(w_in[:, gates_at:gates_end], ((0, 0), (0, LANES - 2 * GLA_GATE_RANK))).astype(BF16)
    w_up = jnp.zeros((LANES, 2 * GLA_QK), F32)
    w_up = w_up.at[:GLA_GATE_RANK, :GLA_QK].set(p["w_gate_up_f"])
    w_up = w_up.at[GLA_GATE_RANK:2 * GLA_GATE_RANK, GLA_QK:].set(p["w_gate_up_b"])
    w["w_up"] = w_up.astype(BF16)
    w["b_g"] = jnp.concatenate([p["b_gate_f"], p["b_gate_b"]])[None, :]
    for name in ("w_out", "w_xq", "w_xkv", "w_xo", "w_ffn_gate", "w_ffn_up", "w_ffn_down"):
        w[name] = p[name].astype(BF16)
    for name in ("norm_mix_pre", "gla_norm_w", "lambda_q1", "lambda_k1", "lambda_q2", "lambda_k2",
                 "diff_subln_w", "norm_mix_post", "norm_xattn_pre", "norm_mem", "norm_xattn_post",
                 "norm_ffn_pre", "norm_ffn_post"):
        w[name] = p[name][None, :]
    return w


def _encoder_layer(x, mem, w, lam_init):
    batch, seq, _ = x.shape
    assert seq % TOKEN_TILE == 0 and seq % GLA_BLOCK == 0 and mem.shape[1] == MEM_LEN
    x2d = x.reshape(batch * seq, D_MODEL)
    mem_k, mem_v = _mem_kv(mem.reshape(batch * MEM_LEN, D_MODEL), w["norm_mem"], w["w_xkv"])
    f, b, v, og, decf, decb, dq, dk, dv = _in_proj(x2d, seq, w)
    o_f, o_b = _gla(f, b, v, decf, decb, batch, seq)
    o_diff = _diff_attn(dq, dk, dv, w, batch, seq, lam_init)
    x2 = _mix_xattn(x2d, o_f, o_b, og, o_diff, mem_k, mem_v, seq, w)
    return _ffn(x2, w).reshape(batch, seq, D_MODEL)


def kernel(x_prompt, x_sample, mem_prompt, mem_sample, norm_mix_pre, w_in, w_gate_up_f, b_gate_f,
           w_gate_up_b, b_gate_b, gla_norm_w, lambda_q1, lambda_k1, lambda_q2, lambda_k2, diff_subln_w,
           w_out, norm_mix_post, norm_xattn_pre, norm_mem, w_xq, w_xkv, w_xo, norm_xattn_post,
           norm_ffn_pre, w_ffn_gate, w_ffn_up, w_ffn_down, norm_ffn_post):
    stacked = dict(
        norm_mix_pre=norm_mix_pre, w_in=w_in, w_gate_up_f=w_gate_up_f, b_gate_f=b_gate_f,
        w_gate_up_b=w_gate_up_b, b_gate_b=b_gate_b, gla_norm_w=gla_norm_w, lambda_q1=lambda_q1,
        lambda_k1=lambda_k1, lambda_q2=lambda_q2, lambda_k2=lambda_k2, diff_subln_w=diff_subln_w,
        w_out=w_out, norm_mix_post=norm_mix_post, norm_xattn_pre=norm_xattn_pre, norm_mem=norm_mem,
        w_xq=w_xq, w_xkv=w_xkv, w_xo=w_xo, norm_xattn_post=norm_xattn_post,
        norm_ffn_pre=norm_ffn_pre, w_ffn_gate=w_ffn_gate, w_ffn_up=w_ffn_up, w_ffn_down=w_ffn_down,
        norm_ffn_post=norm_ffn_post)
    y_prompt, y_sample = x_prompt, x_sample
    for layer in range(w_in.shape[0]):
        lam_init = 0.8 - 0.6 * math.exp(-0.3 * layer)
        w = _prepare_weights({name: t[layer] for name, t in stacked.items()})
        y_prompt = _encoder_layer(y_prompt, mem_prompt, w, lam_init)
        y_sample = _encoder_layer(y_sample, mem_sample, w, lam_init)
    return (y_prompt, y_sample)
```

```python
import functools
import math

import numpy as np
import jax
import jax.numpy as jnp
from jax import lax
from jax.experimental import pallas as pl
from jax.experimental.pallas import tpu as pltpu

F32 = jnp.float32
BF16 = jnp.bfloat16

D_MODEL = 1024
GLA_HEADS = 4
GLA_DK = 64
GLA_DV = 128
GLA_QK = GLA_HEADS * GLA_DK
GLA_V = GLA_HEADS * GLA_DV
GLA_GATE_RANK = 16
GLA_GATE_NORMALIZER = 16.0
GLA_CHUNK = 64
DIFF_HEADS = 4
DIFF_DH = 64
DIFF_QK = DIFF_HEADS * 2 * DIFF_DH
DIFF_V = DIFF_HEADS * 2 * DIFF_DH
ROT_DIM = DIFF_DH // 4
ROPE_THETA = 500000.0
MEM_LEN = 256
XATTN_HEADS = 4
XATTN_DH = D_MODEL // XATTN_HEADS
D_FF = 2816
RMS_EPS = 1e-6

LANES = 128
TOKEN_TILE = 512
GLA_BLOCK = 512
ATTN_Q_TILE = 512
ATTN_K_TILE = 2048
ATTN_KEY_BLOCK = 256
LOG2E = math.log2(math.e)
VMEM_LIMIT = 56 * 1024 * 1024

NT_DIMS = (((1,), (1,)), ((), ()))
TN_DIMS = (((0,), (0,)), ((), ()))


def _rms(x, w):
    return x * lax.rsqrt(jnp.mean(x * x, axis=-1, keepdims=True) + RMS_EPS) * w


def _dot(a, b):
    return jnp.dot(a, b, preferred_element_type=F32)


def _dot_nt(a, b):
    return lax.dot_general(a, b, NT_DIMS, preferred_element_type=F32)


def _dot_tn(a, b):
    return lax.dot_general(a, b, TN_DIMS, preferred_element_type=F32)


def _const_spec(shape):
    return pl.BlockSpec(shape, lambda *_: (0,) * len(shape))


def _params(semantics):
    return pltpu.CompilerParams(dimension_semantics=semantics, vmem_limit_bytes=VMEM_LIMIT)


def _mem_kv_kernel(m_ref, nw_ref, w_ref, k_ref, v_ref):
    m = _rms(m_ref[...], nw_ref[...]).astype(BF16)
    kv = _dot(m, w_ref[...])
    k_ref[...] = kv[:, :D_MODEL].astype(BF16)
    v_ref[...] = kv[:, D_MODEL:].astype(BF16)


def _mem_kv(mem2d, norm_w, w_xkv):
    rows = mem2d.shape[0]
    out = jax.ShapeDtypeStruct((rows, D_MODEL), BF16)
    return pl.pallas_call(
        _mem_kv_kernel,
        grid=(rows // MEM_LEN,),
        in_specs=[pl.BlockSpec((MEM_LEN, D_MODEL), lambda i: (i, 0)),
                  _const_spec((1, D_MODEL)),
                  _const_spec((D_MODEL, 2 * D_MODEL))],
        out_specs=[pl.BlockSpec((MEM_LEN, D_MODEL), lambda i: (i, 0))] * 2,
        out_shape=[out, out],
        compiler_params=_params(("parallel",)),
        name="mem_kv",
    )(mem2d, norm_w, w_xkv)


def _inproj_kernel(x_ref, nw_ref, wa_ref, wg_ref, wup_ref, bg_ref, tri_ref, cos_ref, sin_ref,
                   f_ref, b_ref, v_ref, og_ref, decf_ref, decb_ref, dq_ref, dk_ref, dv_ref):
    tm = x_ref.shape[0]
    h = _rms(x_ref[...], nw_ref[...]).astype(BF16)

    gdown = _dot(h, wg_ref[...]).astype(BF16)
    pre = _dot(gdown, wup_ref[...]) + bg_ref[...]
    g = (jnp.minimum(pre, 0.0) - jnp.log1p(jnp.exp(-jnp.abs(pre)))) * (1.0 / GLA_GATE_NORMALIZER)
    g_hi = g.astype(BF16)
    g_lo = (g - g_hi.astype(F32)).astype(BF16)
    g_split = jnp.concatenate([g_hi, g_lo], axis=1)
    tri = tri_ref[...]
    half = tri.shape[0]
    pre_sum = jnp.concatenate([_dot(tri, g_split[r:r + half]) for r in range(0, tm, half)], axis=0)
    prefix = pre_sum[:, :2 * GLA_QK] + pre_sum[:, 2 * GLA_QK:]
    totals = [prefix[c + GLA_CHUNK - 1:c + GLA_CHUNK] for c in range(0, tm, GLA_CHUNK)]
    total = jnp.concatenate([jnp.broadcast_to(t, (GLA_CHUNK, 2 * GLA_QK)) for t in totals], axis=0)
    dec = jnp.exp(jnp.concatenate(totals, axis=0))
    pf, pb = prefix[:, :GLA_QK], prefix[:, GLA_QK:]
    tf, tb = total[:, :GLA_QK], total[:, GLA_QK:]
    gb = g[:, GLA_QK:]

    qk = _dot(h, wa_ref[:, 0:2 * GLA_QK])
    q = qk[:, :GLA_QK] * (GLA_DK ** -0.5)
    k = qk[:, GLA_QK:]
    for cum, rest, o_ref in ((pf, tf - pf, f_ref), (tb - pb + gb, pb - gb, b_ref)):
        o_ref[:, 0:GLA_QK] = (q * jnp.exp(cum)).astype(BF16)
        o_ref[:, GLA_QK:2 * GLA_QK] = (k * jnp.exp(-cum)).astype(BF16)
        o_ref[:, 2 * GLA_QK:3 * GLA_QK] = (k * jnp.exp(rest)).astype(BF16)
    decf_ref[...] = dec[:, :GLA_QK]
    decb_ref[...] = dec[:, GLA_QK:]

    off = 2 * GLA_QK
    v_ref[...] = _dot(h, wa_ref[:, off:off + GLA_V]).astype(BF16)
    off += GLA_V
    og_ref[...] = _dot(h, wa_ref[:, off:off + GLA_V]).astype(BF16)
    off += GLA_V

    cos = cos_ref[...]
    sin = sin_ref[...]
    lane = lax.broadcasted_iota(jnp.int32, (tm, LANES), 1)
    second_half = (lane & (DIFF_DH - 1)) >= (ROT_DIM // 2)

    def rope(t):
        slabs = []
        for s in range(t.shape[1] // LANES):
            ts = t[:, s * LANES:(s + 1) * LANES]
            partner = jnp.where(second_half, pltpu.roll(ts, ROT_DIM // 2, 1),
                                pltpu.roll(ts, LANES - ROT_DIM // 2, 1))
            slabs.append(ts * cos + partner * sin)
        return jnp.concatenate(slabs, axis=1)

    dq = _dot(h, wa_ref[:, off:off + DIFF_QK])
    dq_ref[...] = (rope(dq) * (DIFF_DH ** -0.5 * LOG2E)).astype(BF16)
    off += DIFF_QK
    dk = _dot(h, wa_ref[:, off:off + DIFF_QK])
    dk_ref[...] = rope(dk).astype(BF16)
    off += DIFF_QK
    dv_ref[...] = _dot(h, wa_ref[:, off:off + DIFF_V]).astype(BF16)


def _chunk_prefix_matrix(rows):
    r = np.arange(rows)[:, None]
    c = np.arange(rows)[None, :]
    return jnp.asarray(((r // GLA_CHUNK) == (c // GLA_CHUNK)) & (c <= r), BF16)


def _rope_tables(seq):
    inv = ROPE_THETA ** (-jnp.arange(0, ROT_DIM, 2, dtype=F32) / ROT_DIM)
    ang = jnp.arange(seq).astype(F32)[:, None] * inv[None, :]
    cos, sin = jnp.cos(ang), jnp.sin(ang)
    rest = DIFF_DH - ROT_DIM
    cos64 = jnp.concatenate([cos, cos, jnp.ones((seq, rest), F32)], axis=1)
    sin64 = jnp.concatenate([-sin, sin, jnp.zeros((seq, rest), F32)], axis=1)
    reps = LANES // DIFF_DH
    return jnp.tile(cos64, (1, reps)), jnp.tile(sin64, (1, reps))


def _in_proj(x2d, seq, w):
    tokens = x2d.shape[0]
    tm = TOKEN_TILE
    nchunk = tm // GLA_CHUNK
    tri = _chunk_prefix_matrix(tm // 2)
    cos, sin = _rope_tables(seq)
    pos_blocks = seq // tm
    row = lambda i: (i, 0)
    pos = lambda i: (i % pos_blocks, 0)

    def tok(width, dtype=BF16):
        return jax.ShapeDtypeStruct((tokens, width), dtype)

    dec = jax.ShapeDtypeStruct((tokens // GLA_CHUNK, GLA_QK), F32)
    return pl.pallas_call(
        _inproj_kernel,
        grid=(tokens // tm,),
        in_specs=[pl.BlockSpec((tm, D_MODEL), row),
                  _const_spec((1, D_MODEL)),
                  _const_spec(w["w_a"].shape),
                  _const_spec(w["w_g"].shape),
                  _const_spec(w["w_up"].shape),
                  _const_spec(w["b_g"].shape),
                  _const_spec(tri.shape),
                  pl.BlockSpec((tm, LANES), pos),
                  pl.BlockSpec((tm, LANES), pos)],
        out_specs=[pl.BlockSpec((tm, 3 * GLA_QK), row),
                   pl.BlockSpec((tm, 3 * GLA_QK), row),
                   pl.BlockSpec((tm, GLA_V), row),
                   pl.BlockSpec((tm, GLA_V), row),
                   pl.BlockSpec((nchunk, GLA_QK), row),
                   pl.BlockSpec((nchunk, GLA_QK), row),
                   pl.BlockSpec((tm, DIFF_QK), row),
                   pl.BlockSpec((tm, DIFF_QK), row),
                   pl.BlockSpec((tm, DIFF_V), row)],
        out_shape=[tok(3 * GLA_QK), tok(3 * GLA_QK), tok(GLA_V), tok(GLA_V), dec, dec,
                   tok(DIFF_QK), tok(DIFF_QK), tok(DIFF_V)],
        compiler_params=_params(("parallel",)),
        name="in_proj",
    )(x2d, w["norm_mix_pre"], w["w_a"], w["w_g"], w["w_up"], w["b_g"], tri, cos, sin)


def _gla_kernel(f_ref, b_ref, vf_ref, vb_ref, decf_ref, decb_ref, of_ref, ob_ref, sf_ref, sb_ref):
    nchunk = f_ref.shape[0] // GLA_CHUNK

    @pl.when(pl.program_id(1) == 0)
    def _():
        sf_ref[...] = jnp.zeros_like(sf_ref)
        sb_ref[...] = jnp.zeros_like(sb_ref)

    qk_head = lax.broadcasted_iota(jnp.int32, (GLA_CHUNK, GLA_QK), 1) // GLA_DK
    st_head = lax.broadcasted_iota(jnp.int32, (GLA_DV, GLA_QK), 1) // GLA_DK
    a_row = lax.broadcasted_iota(jnp.int32, (GLA_HEADS * GLA_CHUNK, GLA_CHUNK), 0) % GLA_CHUNK
    a_col = lax.broadcasted_iota(jnp.int32, (GLA_HEADS * GLA_CHUNK, GLA_CHUNK), 1)

    def chunk_step(x_ref, v_ref, dec_ref, o_ref, s_ref, c, keep):
        rows = slice(c * GLA_CHUNK, (c + 1) * GLA_CHUNK)
        qe = x_ref[rows, 0:GLA_QK]
        ke = x_ref[rows, GLA_QK:2 * GLA_QK]
        kd = x_ref[rows, 2 * GLA_QK:3 * GLA_QK]
        v = v_ref[rows, :]
        dec = dec_ref[c:c + 1, :]
        qm = jnp.concatenate([jnp.where(qk_head == h, qe, jnp.zeros_like(qe))
                              for h in range(GLA_HEADS)], axis=0)
        a = jnp.where(keep, _dot_nt(qm, ke), 0.0).astype(BF16)
        st = s_ref[...]
        inter = _dot_nt(qm, st.astype(BF16))
        outs = []
        for h in range(GLA_HEADS):
            hrows = slice(h * GLA_CHUNK, (h + 1) * GLA_CHUNK)
            outs.append(_dot(a[hrows], v[:, h * GLA_DV:(h + 1) * GLA_DV]) + inter[hrows])
        o_ref[rows, :] = jnp.concatenate(outs, axis=1).astype(BF16)
        ut = _dot_tn(v, kd)
        upd = jnp.zeros_like(st)
        for h in range(GLA_HEADS):
            upd = upd + jnp.where(st_head == h, ut[h * GLA_DV:(h + 1) * GLA_DV], 0.0)
        s_ref[...] = dec * st + upd

    for c in range(nchunk):
        chunk_step(f_ref, vf_ref, decf_ref, of_ref, sf_ref, c, a_col <= a_row)
        chunk_step(b_ref, vb_ref, decb_ref, ob_ref, sb_ref, nchunk - 1 - c, a_col >= a_row)


def _gla(f, b, v, decf, decb, batch, seq):
    tokens = f.shape[0]
    lb = GLA_BLOCK
    nblk = seq // lb
    nchunk = lb // GLA_CHUNK
    fwd = lambda bi, i: (bi * nblk + i, 0)
    bwd = lambda bi, i: (bi * nblk + nblk - 1 - i, 0)
    out = jax.ShapeDtypeStruct((tokens, GLA_V), BF16)
    return pl.pallas_call(
        _gla_kernel,
        grid=(batch, nblk),
        in_specs=[pl.BlockSpec((lb, 3 * GLA_QK), fwd),
                  pl.BlockSpec((lb, 3 * GLA_QK), bwd),
                  pl.BlockSpec((lb, GLA_V), fwd),
                  pl.BlockSpec((lb, GLA_V), bwd),
                  pl.BlockSpec((nchunk, GLA_QK), fwd),
                  pl.BlockSpec((nchunk, GLA_QK), bwd)],
        out_specs=[pl.BlockSpec((lb, GLA_V), fwd),
                   pl.BlockSpec((lb, GLA_V), bwd)],
        out_shape=[out, out],
        scratch_shapes=[pltpu.VMEM((GLA_DV, GLA_QK), F32),
                        pltpu.VMEM((GLA_DV, GLA_QK), F32)],
        compiler_params=_params(("parallel", "arbitrary")),
        name="gla",
    )(f, b, v, v, decf, decb)


def _diff_kernel(q_ref, k_ref, v_ref, lq1_ref, lk1_ref, lq2_ref, lk2_ref, w_ref, o_ref,
                 qm_ref, m_ref, l_ref, acc_ref, s_ref, *, lam_init):
    ki = pl.program_id(2)
    tq = q_ref.shape[0]
    tk = k_ref.shape[0]
    width = 2 * DIFF_DH

    @pl.when(ki == 0)
    def _():
        m_ref[...] = jnp.full_like(m_ref, -jnp.inf)
        l_ref[...] = jnp.zeros_like(l_ref)
        acc_ref[...] = jnp.zeros_like(acc_ref)
        first = lax.broadcasted_iota(jnp.int32, (tq, width), 1) < DIFF_DH
        for h in range(DIFF_HEADS):
            qh = q_ref[:, h * width:(h + 1) * width]
            zero = jnp.zeros_like(qh)
            qm_ref[2 * h] = jnp.where(first, qh, zero)
            qm_ref[2 * h + 1] = jnp.where(first, zero, qh)

    ones = jnp.ones((ATTN_KEY_BLOCK, width), BF16)
    nstream = 2 * DIFF_HEADS

    def scores(s_idx):
        h = s_idx // 2
        s = _dot_nt(qm_ref[s_idx], k_ref[:, h * width:(h + 1) * width])
        s_ref[s_idx % 2] = s
        m_prev = m_ref[s_idx]
        m_next = jnp.maximum(m_prev, jnp.max(s, axis=1, keepdims=True))
        m_ref[s_idx] = m_next
        return m_prev, m_next

    maxes = scores(0)
    for s_idx in range(nstream):
        m_prev, m_next = maxes
        if s_idx + 1 < nstream:
            maxes = scores(s_idx + 1)
        h = s_idx // 2
        vh = v_ref[:, h * width:(h + 1) * width]
        m_wide = jnp.tile(m_next, (1, ATTN_KEY_BLOCK // LANES))
        pv = jnp.zeros((tq, 2 * width), F32)
        for k0 in range(0, tk, ATTN_KEY_BLOCK):
            keys = slice(k0, k0 + ATTN_KEY_BLOCK)
            pb = jnp.exp2(s_ref[s_idx % 2, :, keys] - m_wide).astype(BF16)
            pv = pv + _dot(pb, jnp.concatenate([vh[keys], ones], axis=1))
        alpha = jnp.exp2(m_prev - m_next)
        acc_ref[s_idx] = alpha * acc_ref[s_idx] + pv[:, :width]
        l_ref[s_idx] = alpha * l_ref[s_idx] + pv[:, width:]

    @pl.when(ki == pl.num_programs(2) - 1)
    def _():
        lam = (jnp.exp(jnp.sum(lq1_ref[...] * lk1_ref[...], axis=1, keepdims=True))
               - jnp.exp(jnp.sum(lq2_ref[...] * lk2_ref[...], axis=1, keepdims=True)) + lam_init)
        for h in range(DIFF_HEADS):
            o1 = acc_ref[2 * h] / l_ref[2 * h]
            o2 = acc_ref[2 * h + 1] / l_ref[2 * h + 1]
            o = _rms(o1 - lam * o2, w_ref[...]) * (1.0 - lam_init)
            o_ref[:, h * width:(h + 1) * width] = o.astype(BF16)


def _diff_attn(dq, dk, dv, w, batch, seq, lam_init):
    tokens = dq.shape[0]
    tq = min(ATTN_Q_TILE, seq)
    tk = min(ATTN_K_TILE, seq)
    nq, nk = seq // tq, seq // tk
    qmap = lambda b, qi, ki: (b * nq + qi, 0)
    kmap = lambda b, qi, ki: (b * nk + ki, 0)
    nstream = 2 * DIFF_HEADS
    vec = _const_spec((1, DIFF_DH))
    return pl.pallas_call(
        functools.partial(_diff_kernel, lam_init=lam_init),
        grid=(batch, nq, nk),
        in_specs=[pl.BlockSpec((tq, DIFF_QK), qmap),
                  pl.BlockSpec((tk, DIFF_QK), kmap),
                  pl.BlockSpec((tk, DIFF_V), kmap),
                  vec, vec, vec, vec,
                  _const_spec((1, 2 * DIFF_DH))],
        out_specs=pl.BlockSpec((tq, DIFF_V), qmap),
        out_shape=jax.ShapeDtypeStruct((tokens, DIFF_V), BF16),
        scratch_shapes=[pltpu.VMEM((nstream, tq, 2 * DIFF_DH), BF16),
                        pltpu.VMEM((nstream, tq, LANES), F32),
                        pltpu.VMEM((nstream, tq, LANES), F32),
                        pltpu.VMEM((nstream, tq, 2 * DIFF_DH), F32),
                        pltpu.VMEM((2, tq, tk), F32)],
        compiler_params=_params(("parallel", "parallel", "arbitrary")),
        name="diff_attn",
    )(dq, dk, dv, w["lambda_q1"], w["lambda_k1"], w["lambda_q2"], w["lambda_k2"], w["diff_subln_w"])


def _mix_xattn_kernel(x_ref, of_ref, ob_ref, og_ref, od_ref, k_ref, v_ref,
                      gnw_ref, wout_ref, npost_ref, nxpre_ref, wxq_ref, wxo_ref, nxpost_ref, y_ref):
    o = of_ref[...].astype(F32) + ob_ref[...].astype(F32)
    og = og_ref[...].astype(F32)
    gate = og * (1.0 / (1.0 + jnp.exp(-og)))
    heads = []
    for h in range(GLA_HEADS):
        cols = slice(h * GLA_DV, (h + 1) * GLA_DV)
        heads.append((_rms(o[:, cols], gnw_ref[...]) * gate[:, cols]).astype(BF16))
    mix_in = jnp.concatenate(heads + [od_ref[...]], axis=1)
    mix = _dot(mix_in, wout_ref[...])
    x1 = x_ref[...] + _rms(mix, npost_ref[...])

    h2 = _rms(x1, nxpre_ref[...]).astype(BF16)
    q = (_dot(h2, wxq_ref[...]) * (XATTN_DH ** -0.5)).astype(BF16)
    outs = []
    for h in range(XATTN_HEADS):
        cols = slice(h * XATTN_DH, (h + 1) * XATTN_DH)
        s = _dot_nt(q[:, cols], k_ref[:, cols])
        p = jnp.exp(s - jnp.max(s, axis=1, keepdims=True))
        denom = jnp.sum(p, axis=1, keepdims=True)
        outs.append((_dot(p.astype(BF16), v_ref[:, cols]) / denom).astype(BF16))
    xo = _dot(jnp.concatenate(outs, axis=1), wxo_ref[...])
    y_ref[...] = x1 + _rms(xo, nxpost_ref[...])


def _mix_xattn(x2d, o_f, o_b, og, o_diff, mem_k, mem_v, seq, w):
    tokens = x2d.shape[0]
    tm = TOKEN_TILE
    per_seq = seq // tm
    row = lambda i: (i, 0)
    memmap = lambda i: (i // per_seq, 0)
    half = pl.BlockSpec((tm, GLA_V), row)
    sq = _const_spec((D_MODEL, D_MODEL))
    nv = _const_spec((1, D_MODEL))
    return pl.pallas_call(
        _mix_xattn_kernel,
        grid=(tokens // tm,),
        in_specs=[pl.BlockSpec((tm, D_MODEL), row), half, half, half, half,
                  pl.BlockSpec((MEM_LEN, D_MODEL), memmap),
                  pl.BlockSpec((MEM_LEN, D_MODEL), memmap),
                  _const_spec((1, GLA_DV)), sq, nv, nv, sq, sq, nv],
        out_specs=pl.BlockSpec((tm, D_MODEL), row),
        out_shape=jax.ShapeDtypeStruct((tokens, D_MODEL), F32),
        compiler_params=_params(("parallel",)),
        name="mix_xattn",
    )(x2d, o_f, o_b, og, o_diff, mem_k, mem_v, w["gla_norm_w"], w["w_out"], w["norm_mix_post"],
      w["norm_xattn_pre"], w["w_xq"], w["w_xo"], w["norm_xattn_post"])


def _ffn_kernel(x_ref, npre_ref, wg_ref, wu_ref, wd_ref, npost_ref, y_ref):
    x = x_ref[...]
    h = _rms(x, npre_ref[...]).astype(BF16)
    g = _dot(h, wg_ref[...])
    u = _dot(h, wu_ref[...])
    a = (g * (1.0 / (1.0 + jnp.exp(-g))) * u).astype(BF16)
    y_ref[...] = x + _rms(_dot(a, wd_ref[...]), npost_ref[...])


def _ffn(x2d, w):
    tokens = x2d.shape[0]
    tm = TOKEN_TILE
    row = lambda i: (i, 0)
    nv = _const_spec((1, D_MODEL))
    return pl.pallas_call(
        _ffn_kernel,
        grid=(tokens // tm,),
        in_specs=[pl.BlockSpec((tm, D_MODEL), row), nv,
                  _const_spec((D_MODEL, D_FF)), _const_spec((D_MODEL, D_FF)),
                  _const_spec((D_FF, D_MODEL)), nv],
        out_specs=pl.BlockSpec((tm, D_MODEL), row),
        out_shape=jax.ShapeDtypeStruct((tokens, D_MODEL), F32),
        compiler_params=_params(("parallel",)),
        name="ffn",
    )(x2d, w["norm_ffn_pre"], w["w_ffn_gate"], w["w_ffn_up"], w["w_ffn_down"], w["norm_ffn_post"])


def _prepare_weights(p):
    w_in = p["w_in"]
    gates_at = 2 * GLA_QK + GLA_V
    gates_end = gates_at + 2 * GLA_GATE_RANK
    w = {}
    w["w_a"] = jnp.concatenate([w_in[:, :gates_at], w_in[:, gates_end:]], axis=1).astype(BF16)
    w["w_g"] = jnp.pad(w_in[:, gates_at:gates_end], ((0, 0), (0, LANES - 2 * GLA_GATE_RANK))).astype(BF16)
    w_up = jnp.zeros((LANES, 2 * GLA_QK), F32)
    w_up = w_up.at[:GLA_GATE_RANK, :GLA_QK].set(p["w_gate_up_f"])
    w_up = w_up.at[GLA_GATE_RANK:2 * GLA_GATE_RANK, GLA_QK:].set(p["w_gate_up_b"])
    w["w_up"] = w_up.astype(BF16)
    w["b_g"] = jnp.concatenate([p["b_gate_f"], p["b_gate_b"]])[None, :]
    for name in ("w_out", "w_xq", "w_xkv", "w_xo", "w_ffn_gate", "w_ffn_up", "w_ffn_down"):
        w[name] = p[name].astype(BF16)
    for name in ("norm_mix_pre", "gla_norm_w", "lambda_q1", "lambda_k1", "lambda_q2", "lambda_k2",
                 "diff_subln_w", "norm_mix_post", "norm_xattn_pre", "norm_mem", "norm_xattn_post",
                 "norm_ffn_pre", "norm_ffn_post"):
        w[name] = p[name][None, :]
    return w


def _encoder_layer(x, mem, w, lam_init):
    batch, seq, _ = x.shape
    assert seq % TOKEN_TILE == 0 and seq % GLA_BLOCK == 0 and mem.shape[1] == MEM_LEN
    x2d = x.reshape(batch * seq, D_MODEL)
    mem_k, mem_v = _mem_kv(mem.reshape(batch * MEM_LEN, D_MODEL), w["norm_mem"], w["w_xkv"])
    f, b, v, og, decf, decb, dq, dk, dv = _in_proj(x2d, seq, w)
    o_f, o_b = _gla(f, b, v, decf, decb, batch, seq)
    o_diff = _diff_attn(dq, dk, dv, w, batch, seq, lam_init)
    x2 = _mix_xattn(x2d, o_f, o_b, og, o_diff, mem_k, mem_v, seq, w)
    return _ffn(x2, w).reshape(batch, seq, D_MODEL)


def kernel(x_prompt, x_sample, mem_prompt, mem_sample, norm_mix_pre, w_in, w_gate_up_f, b_gate_f,
           w_gate_up_b, b_gate_b, gla_norm_w, lambda_q1, lambda_k1, lambda_q2, lambda_k2, diff_subln_w,
           w_out, norm_mix_post, norm_xattn_pre, norm_mem, w_xq, w_xkv, w_xo, norm_xattn_post,
           norm_ffn_pre, w_ffn_gate, w_ffn_up, w_ffn_down, norm_ffn_post):
    stacked = dict(
        norm_mix_pre=norm_mix_pre, w_in=w_in, w_gate_up_f=w_gate_up_f, b_gate_f=b_gate_f,
        w_gate_up_b=w_gate_up_b, b_gate_b=b_gate_b, gla_norm_w=gla_norm_w, lambda_q1=lambda_q1,
        lambda_k1=lambda_k1, lambda_q2=lambda_q2, lambda_k2=lambda_k2, diff_subln_w=diff_subln_w,
        w_out=w_out, norm_mix_post=norm_mix_post, norm_xattn_pre=norm_xattn_pre, norm_mem=norm_mem,
        w_xq=w_xq, w_xkv=w_xkv, w_xo=w_xo, norm_xattn_post=norm_xattn_post,
        norm_ffn_pre=norm_ffn_pre, w_ffn_gate=w_ffn_gate, w_ffn_up=w_ffn_up, w_ffn_down=w_ffn_down,
        norm_ffn_post=norm_ffn_post)
    y_prompt, y_sample = x_prompt, x_sample
    for layer in range(w_in.shape[0]):
        lam_init = 0.8 - 0.6 * math.exp(-0.3 * layer)
        w = _prepare_weights({name: t[layer] for name, t in stacked.items()})
        y_prompt = _encoder_layer(y_prompt, mem_prompt, w, lam_init)
        y_sample = _encoder_layer(y_sample, mem_sample, w, lam_init)
    return (y_prompt, y_sample)
```

```python
import functools
import math

import numpy as np
import jax
import jax.numpy as jnp
from jax import lax
from jax.experimental import pallas as pl
from jax.experimental.pallas import tpu as pltpu

F32 = jnp.float32
BF16 = jnp.bfloat16

D_MODEL = 1024
GLA_HEADS = 4
GLA_DK = 64
GLA_DV = 128
GLA_QK = GLA_HEADS * GLA_DK
GLA_V = GLA_HEADS * GLA_DV
GLA_GATE_RANK = 16
GLA_GATE_NORMALIZER = 16.0
GLA_CHUNK = 64
DIFF_HEADS = 4
DIFF_DH = 64
DIFF_QK = DIFF_HEADS * 2 * DIFF_DH
DIFF_V = DIFF_HEADS * 2 * DIFF_DH
ROT_DIM = DIFF_DH // 4
ROPE_THETA = 500000.0
MEM_LEN = 256
XATTN_HEADS = 4
XATTN_DH = D_MODEL // XATTN_HEADS
D_FF = 2816
RMS_EPS = 1e-6

LANES = 128
TOKEN_TILE = 512
GLA_BLOCK = 512
ATTN_Q_TILE = 512
ATTN_K_TILE = 2048
MIX_ROW_SPLIT = 2
ATTN_KEY_BLOCK = 256
LOG2E = math.log2(math.e)
VMEM_LIMIT = 56 * 1024 * 1024

NT_DIMS = (((1,), (1,)), ((), ()))
TN_DIMS = (((0,), (0,)), ((), ()))


def _rms(x, w):
    return x * lax.rsqrt(jnp.mean(x * x, axis=-1, keepdims=True) + RMS_EPS) * w


def _dot(a, b):
    return jnp.dot(a, b, preferred_element_type=F32)


def _dot_nt(a, b):
    return lax.dot_general(a, b, NT_DIMS, preferred_element_type=F32)


def _dot_tn(a, b):
    return lax.dot_general(a, b, TN_DIMS, preferred_element_type=F32)


def _const_spec(shape):
    return pl.BlockSpec(shape, lambda *_: (0,) * len(shape), pipeline_mode=pl.Buffered(1))


def _params(semantics):
    return pltpu.CompilerParams(dimension_semantics=semantics, vmem_limit_bytes=VMEM_LIMIT)


def _mem_kv_kernel(m_ref, nw_ref, w_ref, k_ref, v_ref):
    m = _rms(m_ref[...], nw_ref[...]).astype(BF16)
    kv = _dot(m, w_ref[...])
    k_ref[...] = kv[:, :D_MODEL].astype(BF16)
    v_ref[...] = kv[:, D_MODEL:].astype(BF16)


def _mem_kv(mem2d, norm_w, w_xkv):
    rows = mem2d.shape[0]
    out = jax.ShapeDtypeStruct((rows, D_MODEL), BF16)
    return pl.pallas_call(
        _mem_kv_kernel,
        grid=(rows // MEM_LEN,),
        in_specs=[pl.BlockSpec((MEM_LEN, D_MODEL), lambda i: (i, 0)),
                  _const_spec((1, D_MODEL)),
                  _const_spec((D_MODEL, 2 * D_MODEL))],
        out_specs=[pl.BlockSpec((MEM_LEN, D_MODEL), lambda i: (i, 0))] * 2,
        out_shape=[out, out],
        compiler_params=_params(("parallel",)),
        name="mem_kv",
    )(mem2d, norm_w, w_xkv)


def _inproj_kernel(x_ref, nw_ref, wa_ref, wg_ref, wup_ref, bg_ref, tri_ref, cos_ref, sin_ref,
                   f_ref, b_ref, v_ref, og_ref, decf_ref, decb_ref, dq_ref, dk_ref, dv_ref):
    tm = x_ref.shape[0]
    h = _rms(x_ref[...], nw_ref[...]).astype(BF16)

    gdown = _dot(h, wg_ref[...]).astype(BF16)
    pre = _dot(gdown, wup_ref[...]) + bg_ref[...]
    g = (jnp.minimum(pre, 0.0) - jnp.log1p(jnp.exp(-jnp.abs(pre)))) * (1.0 / GLA_GATE_NORMALIZER)
    g_hi = g.astype(BF16)
    g_lo = (g - g_hi.astype(F32)).astype(BF16)
    g_split = jnp.concatenate([g_hi, g_lo], axis=1)
    tri = tri_ref[...]
    half = tri.shape[0]
    pre_sum = jnp.concatenate([_dot(tri, g_split[r:r + half]) for r in range(0, tm, half)], axis=0)
    prefix = pre_sum[:, :2 * GLA_QK] + pre_sum[:, 2 * GLA_QK:]
    totals = [prefix[c + GLA_CHUNK - 1:c + GLA_CHUNK] for c in range(0, tm, GLA_CHUNK)]
    total = jnp.concatenate([jnp.broadcast_to(t, (GLA_CHUNK, 2 * GLA_QK)) for t in totals], axis=0)
    dec = jnp.exp(jnp.concatenate(totals, axis=0))
    pf, pb = prefix[:, :GLA_QK], prefix[:, GLA_QK:]
    tf, tb = total[:, :GLA_QK], total[:, GLA_QK:]
    gb = g[:, GLA_QK:]

    qk = _dot(h, wa_ref[:, 0:2 * GLA_QK])
    q = qk[:, :GLA_QK] * (GLA_DK ** -0.5)
    k = qk[:, GLA_QK:]
    for cum, rest, o_ref in ((pf, tf - pf, f_ref), (tb - pb + gb, pb - gb, b_ref)):
        o_ref[:, 0:GLA_QK] = (q * jnp.exp(cum)).astype(BF16)
        o_ref[:, GLA_QK:2 * GLA_QK] = (k * jnp.exp(-cum)).astype(BF16)
        o_ref[:, 2 * GLA_QK:3 * GLA_QK] = (k * jnp.exp(rest)).astype(BF16)
    decf_ref[0] = dec[:, :GLA_QK].T
    decb_ref[0] = dec[:, GLA_QK:].T

    off = 2 * GLA_QK
    v_ref[...] = _dot(h, wa_ref[:, off:off + GLA_V]).astype(BF16)
    off += GLA_V
    og_ref[...] = _dot(h, wa_ref[:, off:off + GLA_V]).astype(BF16)
    off += GLA_V

    cos = cos_ref[...]
    sin = sin_ref[...]
    lane = lax.broadcasted_iota(jnp.int32, (tm, LANES), 1)
    second_half = (lane & (DIFF_DH - 1)) >= (ROT_DIM // 2)

    def rope(t):
        slabs = []
        for s in range(t.shape[1] // LANES):
            ts = t[:, s * LANES:(s + 1) * LANES]
            partner = jnp.where(second_half, pltpu.roll(ts, ROT_DIM // 2, 1),
                                pltpu.roll(ts, LANES - ROT_DIM // 2, 1))
            slabs.append(ts * cos + partner * sin)
        return jnp.concatenate(slabs, axis=1)

    dq = _dot(h, wa_ref[:, off:off + DIFF_QK])
    dq_ref[...] = (rope(dq) * (DIFF_DH ** -0.5 * LOG2E)).astype(BF16)
    off += DIFF_QK
    dk = _dot(h, wa_ref[:, off:off + DIFF_QK])
    dk_ref[...] = rope(dk).astype(BF16)
    off += DIFF_QK
    dv_ref[...] = _dot(h, wa_ref[:, off:off + DIFF_V]).astype(BF16)


def _chunk_prefix_matrix(rows):
    r = np.arange(rows)[:, None]
    c = np.arange(rows)[None, :]
    return jnp.asarray(((r // GLA_CHUNK) == (c // GLA_CHUNK)) & (c <= r), BF16)


def _rope_tables(seq):
    inv = ROPE_THETA ** (-jnp.arange(0, ROT_DIM, 2, dtype=F32) / ROT_DIM)
    ang = jnp.arange(seq).astype(F32)[:, None] * inv[None, :]
    cos, sin = jnp.cos(ang), jnp.sin(ang)
    rest = DIFF_DH - ROT_DIM
    cos64 = jnp.concatenate([cos, cos, jnp.ones((seq, rest), F32)], axis=1)
    sin64 = jnp.concatenate([-sin, sin, jnp.zeros((seq, rest), F32)], axis=1)
    reps = LANES // DIFF_DH
    return jnp.tile(cos64, (1, reps)), jnp.tile(sin64, (1, reps))


def _in_proj(x2d, seq, w):
    tokens = x2d.shape[0]
    tm = TOKEN_TILE
    nchunk = tm // GLA_CHUNK
    tri = _chunk_prefix_matrix(tm // 2)
    cos, sin = _rope_tables(seq)
    pos_blocks = seq // tm
    row = lambda i: (i, 0)
    pos = lambda i: (i % pos_blocks, 0)

    def tok(width, dtype=BF16):
        return jax.ShapeDtypeStruct((tokens, width), dtype)

    dec = jax.ShapeDtypeStruct((tokens // tm, GLA_QK, nchunk), F32)
    dec_spec = pl.BlockSpec((1, GLA_QK, nchunk), lambda i: (i, 0, 0))
    return pl.pallas_call(
        _inproj_kernel,
        grid=(tokens // tm,),
        in_specs=[pl.BlockSpec((tm, D_MODEL), row),
                  _const_spec((1, D_MODEL)),
                  _const_spec(w["w_a"].shape),
                  _const_spec(w["w_g"].shape),
                  _const_spec(w["w_up"].shape),
                  _const_spec(w["b_g"].shape),
                  _const_spec(tri.shape),
                  pl.BlockSpec((tm, LANES), pos),
                  pl.BlockSpec((tm, LANES), pos)],
        out_specs=[pl.BlockSpec((tm, 3 * GLA_QK), row),
                   pl.BlockSpec((tm, 3 * GLA_QK), row),
                   pl.BlockSpec((tm, GLA_V), row),
                   pl.BlockSpec((tm, GLA_V), row),
                   dec_spec,
                   dec_spec,
                   pl.BlockSpec((tm, DIFF_QK), row),
                   pl.BlockSpec((tm, DIFF_QK), row),
                   pl.BlockSpec((tm, DIFF_V), row)],
        out_shape=[tok(3 * GLA_QK), tok(3 * GLA_QK), tok(GLA_V), tok(GLA_V), dec, dec,
                   tok(DIFF_QK), tok(DIFF_QK), tok(DIFF_V)],
        compiler_params=_params(("parallel",)),
        name="in_proj",
    )(x2d, w["norm_mix_pre"], w["w_a"], w["w_g"], w["w_up"], w["b_g"], tri, cos, sin)


def _gla_kernel(f_ref, b_ref, vf_ref, vb_ref, decf_ref, decb_ref, of_ref, ob_ref, sf_ref, sb_ref):
    nchunk = f_ref.shape[0] // GLA_CHUNK

    @pl.when(pl.program_id(1) == 0)
    def _():
        sf_ref[...] = jnp.zeros_like(sf_ref)
        sb_ref[...] = jnp.zeros_like(sb_ref)

    qk_head = lax.broadcasted_iota(jnp.int32, (GLA_CHUNK, GLA_QK), 1) // GLA_DK
    v_head = lax.broadcasted_iota(jnp.int32, (GLA_CHUNK, GLA_V), 1) // GLA_DV
    s_head = lax.broadcasted_iota(jnp.int32, (GLA_QK, GLA_DV), 0) // GLA_DK
    a_row = lax.broadcasted_iota(jnp.int32, (GLA_CHUNK, GLA_QK), 0)
    a_col = lax.broadcasted_iota(jnp.int32, (GLA_CHUNK, GLA_QK), 1) % GLA_CHUNK

    def chunk_step(x_ref, v_ref, dec_ref, o_ref, s_ref, c, keep):
        rows = slice(c * GLA_CHUNK, (c + 1) * GLA_CHUNK)
        qe = x_ref[rows, 0:GLA_QK]
        ke = x_ref[rows, GLA_QK:2 * GLA_QK]
        kd = x_ref[rows, 2 * GLA_QK:3 * GLA_QK]
        v = v_ref[rows, :]
        dec = dec_ref[0, :, c:c + 1]
        ke_heads = jnp.concatenate([jnp.where(qk_head == h, ke, jnp.zeros_like(ke))
                                    for h in range(GLA_HEADS)], axis=0)
        a = jnp.where(keep, _dot_nt(qe, ke_heads), 0.0).astype(BF16)
        v_heads = jnp.concatenate([jnp.where(v_head == h, v, jnp.zeros_like(v))
                                   for h in range(GLA_HEADS)], axis=0)
        state = s_ref[...]
        state16 = state.astype(BF16)
        s_heads = jnp.concatenate([jnp.where(s_head == h, state16, jnp.zeros_like(state16))
                                   for h in range(GLA_HEADS)], axis=1)
        o_ref[rows, :] = (_dot(a, v_heads) + _dot(qe, s_heads)).astype(BF16)
        upd = jnp.concatenate([_dot_tn(kd[:, h * GLA_DK:(h + 1) * GLA_DK],
                                       v[:, h * GLA_DV:(h + 1) * GLA_DV])
                               for h in range(GLA_HEADS)], axis=0)
        s_ref[...] = dec * state + upd

    for c in range(nchunk):
        chunk_step(f_ref, vf_ref, decf_ref, of_ref, sf_ref, c, a_col <= a_row)
        chunk_step(b_ref, vb_ref, decb_ref, ob_ref, sb_ref, nchunk - 1 - c, a_col >= a_row)


def _gla(f, b, v, decf, decb, batch, seq):
    tokens = f.shape[0]
    lb = GLA_BLOCK
    nblk = seq // lb
    nchunk = lb // GLA_CHUNK
    fwd = lambda bi, i: (bi * nblk + i, 0)
    bwd = lambda bi, i: (bi * nblk + nblk - 1 - i, 0)
    fwd3 = lambda bi, i: (bi * nblk + i, 0, 0)
    bwd3 = lambda bi, i: (bi * nblk + nblk - 1 - i, 0, 0)
    out = jax.ShapeDtypeStruct((tokens, GLA_V), BF16)
    return pl.pallas_call(
        _gla_kernel,
        grid=(batch, nblk),
        in_specs=[pl.BlockSpec((lb, 3 * GLA_QK), fwd),
                  pl.BlockSpec((lb, 3 * GLA_QK), bwd),
                  pl.BlockSpec((lb, GLA_V), fwd),
                  pl.BlockSpec((lb, GLA_V), bwd),
                  pl.BlockSpec((1, GLA_QK, nchunk), fwd3),
                  pl.BlockSpec((1, GLA_QK, nchunk), bwd3)],
        out_specs=[pl.BlockSpec((lb, GLA_V), fwd),
                   pl.BlockSpec((lb, GLA_V), bwd)],
        out_shape=[out, out],
        scratch_shapes=[pltpu.VMEM((GLA_QK, GLA_DV), F32),
                        pltpu.VMEM((GLA_QK, GLA_DV), F32)],
        compiler_params=_params(("parallel", "arbitrary")),
        name="gla",
    )(f, b, v, v, decf, decb)


def _diff_kernel(q_ref, k_ref, v_ref, lq1_ref, lk1_ref, lq2_ref, lk2_ref, w_ref, o_ref,
                 qm_ref, m_ref, l_ref, acc_ref, s_ref, *, lam_init):
    ki = pl.program_id(2)
    tq = q_ref.shape[0]
    tk = k_ref.shape[0]
    width = 2 * DIFF_DH

    @pl.when(ki == 0)
    def _():
        m_ref[...] = jnp.full_like(m_ref, -jnp.inf)
        l_ref[...] = jnp.zeros_like(l_ref)
        acc_ref[...] = jnp.zeros_like(acc_ref)
        first = lax.broadcasted_iota(jnp.int32, (tq, width), 1) < DIFF_DH
        for h in range(DIFF_HEADS):
            qh = q_ref[:, h * width:(h + 1) * width]
            zero = jnp.zeros_like(qh)
            qm_ref[2 * h] = jnp.where(first, qh, zero)
            qm_ref[2 * h + 1] = jnp.where(first, zero, qh)

    ones = jnp.ones((ATTN_KEY_BLOCK, width), BF16)
    nstream = 2 * DIFF_HEADS

    def scores(s_idx):
        h = s_idx // 2
        s = _dot_nt(qm_ref[s_idx], k_ref[:, h * width:(h + 1) * width])
        s_ref[s_idx % 2] = s
        m_prev = m_ref[s_idx]
        m_next = jnp.maximum(m_prev, jnp.max(s, axis=1, keepdims=True))
        m_ref[s_idx] = m_next
        return m_prev, m_next

    maxes = scores(0)
    for s_idx in range(nstream):
        m_prev, m_next = maxes
        if s_idx + 1 < nstream:
            maxes = scores(s_idx + 1)
        h = s_idx // 2
        vh = v_ref[:, h * width:(h + 1) * width]
        m_wide = jnp.tile(m_next, (1, ATTN_KEY_BLOCK // LANES))
        pv = jnp.zeros((tq, 2 * width), F32)
        for k0 in range(0, tk, ATTN_KEY_BLOCK):
            keys = slice(k0, k0 + ATTN_KEY_BLOCK)
            pb = jnp.exp2(s_ref[s_idx % 2, :, keys] - m_wide).astype(BF16)
            pv = pv + _dot(pb, jnp.concatenate([vh[keys], ones], axis=1))
        alpha = jnp.exp2(m_prev - m_next)
        acc_ref[s_idx] = alpha * acc_ref[s_idx] + pv[:, :width]
        l_ref[s_idx] = alpha * l_ref[s_idx] + pv[:, width:]

    @pl.when(ki == pl.num_programs(2) - 1)
    def _():
        lam = (jnp.exp(jnp.sum(lq1_ref[...] * lk1_ref[...], axis=1, keepdims=True))
               - jnp.exp(jnp.sum(lq2_ref[...] * lk2_ref[...], axis=1, keepdims=True)) + lam_init)
        for h in range(DIFF_HEADS):
            o1 = acc_ref[2 * h] / l_ref[2 * h]
            o2 = acc_ref[2 * h + 1] / l_ref[2 * h + 1]
            o = _rms(o1 - lam * o2, w_ref[...]) * (1.0 - lam_init)
            o_ref[:, h * width:(h + 1) * width] = o.astype(BF16)


def _diff_attn(dq, dk, dv, w, batch, seq, lam_init):
    tokens = dq.shape[0]
    tq = min(ATTN_Q_TILE, seq)
    tk = min(ATTN_K_TILE, seq)
    nq, nk = seq // tq, seq // tk
    qmap = lambda b, qi, ki: (b * nq + qi, 0)
    kmap = lambda b, qi, ki: (b * nk + ki, 0)
    nstream = 2 * DIFF_HEADS
    vec = _const_spec((1, DIFF_DH))
    return pl.pallas_call(
        functools.partial(_diff_kernel, lam_init=lam_init),
        grid=(batch, nq, nk),
        in_specs=[pl.BlockSpec((tq, DIFF_QK), qmap),
                  pl.BlockSpec((tk, DIFF_QK), kmap),
                  pl.BlockSpec((tk, DIFF_V), kmap),
                  vec, vec, vec, vec,
                  _const_spec((1, 2 * DIFF_DH))],
        out_specs=pl.BlockSpec((tq, DIFF_V), qmap),
        out_shape=jax.ShapeDtypeStruct((tokens, DIFF_V), BF16),
        scratch_shapes=[pltpu.VMEM((nstream, tq, 2 * DIFF_DH), BF16),
                        pltpu.VMEM((nstream, tq, LANES), F32),
                        pltpu.VMEM((nstream, tq, LANES), F32),
                        pltpu.VMEM((nstream, tq, 2 * DIFF_DH), F32),
                        pltpu.VMEM((2, tq, tk), F32)],
        compiler_params=_params(("parallel", "parallel", "arbitrary")),
        name="diff_attn",
    )(dq, dk, dv, w["lambda_q1"], w["lambda_k1"], w["lambda_q2"], w["lambda_k2"], w["diff_subln_w"])


def _mix_xattn_kernel(x_ref, of_ref, ob_ref, og_ref, od_ref, k_ref, v_ref,
                      gnw_ref, wout_ref, npost_ref, nxpre_ref, wxq_ref, wxo_ref, nxpost_ref, y_ref):
    tm = x_ref.shape[0]
    parts = [slice(r, r + tm // MIX_ROW_SPLIT) for r in range(0, tm, tm // MIX_ROW_SPLIT)]

    def gla_out(rows):
        o = of_ref[rows, :].astype(F32) + ob_ref[rows, :].astype(F32)
        og = og_ref[rows, :].astype(F32)
        gate = og * (1.0 / (1.0 + jnp.exp(-og)))
        heads = []
        for h in range(GLA_HEADS):
            cols = slice(h * GLA_DV, (h + 1) * GLA_DV)
            heads.append((_rms(o[:, cols], gnw_ref[...]) * gate[:, cols]).astype(BF16))
        return jnp.concatenate(heads + [od_ref[rows, :]], axis=1)

    def attend(q):
        outs = []
        for h in range(XATTN_HEADS):
            cols = slice(h * XATTN_DH, (h + 1) * XATTN_DH)
            s = _dot_nt(q[:, cols], k_ref[:, cols])
            p = jnp.exp(s - jnp.max(s, axis=1, keepdims=True))
            denom = jnp.sum(p, axis=1, keepdims=True)
            outs.append((_dot(p.astype(BF16), v_ref[:, cols]) / denom).astype(BF16))
        return jnp.concatenate(outs, axis=1)

    mix_in = [gla_out(rows) for rows in parts]
    mix = [_dot(t, wout_ref[...]) for t in mix_in]
    x1 = [x_ref[rows, :] + _rms(t, npost_ref[...]) for rows, t in zip(parts, mix)]
    h2 = [_rms(t, nxpre_ref[...]).astype(BF16) for t in x1]
    q = [(_dot(t, wxq_ref[...]) * (XATTN_DH ** -0.5)).astype(BF16) for t in h2]
    att = [attend(t) for t in q]
    xo = [_dot(t, wxo_ref[...]) for t in att]
    for rows, res, t in zip(parts, x1, xo):
        y_ref[rows, :] = res + _rms(t, nxpost_ref[...])


def _mix_xattn(x2d, o_f, o_b, og, o_diff, mem_k, mem_v, seq, w):
    tokens = x2d.shape[0]
    tm = MIX_ROW_SPLIT * TOKEN_TILE
    per_seq = seq // tm
    row = lambda i: (i, 0)
    memmap = lambda i: (i // per_seq, 0)
    half = pl.BlockSpec((tm, GLA_V), row)
    sq = _const_spec((D_MODEL, D_MODEL))
    nv = _const_spec((1, D_MODEL))
    return pl.pallas_call(
        _mix_xattn_kernel,
        grid=(tokens // tm,),
        in_specs=[pl.BlockSpec((tm, D_MODEL), row), half, half, half, half,
                  pl.BlockSpec((MEM_LEN, D_MODEL), memmap),
                  pl.BlockSpec((MEM_LEN, D_MODEL), memmap),
                  _const_spec((1, GLA_DV)), sq, nv, nv, sq, sq, nv],
        out_specs=pl.BlockSpec((tm, D_MODEL), row),
        out_shape=jax.ShapeDtypeStruct((tokens, D_MODEL), F32),
        compiler_params=_params(("parallel",)),
        name="mix_xattn",
    )(x2d, o_f, o_b, og, o_diff, mem_k, mem_v, w["gla_norm_w"], w["w_out"], w["norm_mix_post"],
      w["norm_xattn_pre"], w["w_xq"], w["w_xo"], w["norm_xattn_post"])


def _ffn_kernel(x_ref, npre_ref, wg_ref, wu_ref, wd_ref, npost_ref, y_ref):
    tm = x_ref.shape[0]
    parts = [slice(r, r + tm // MIX_ROW_SPLIT) for r in range(0, tm, tm // MIX_ROW_SPLIT)]
    h = [_rms(x_ref[rows, :], npre_ref[...]).astype(BF16) for rows in parts]
    g = [_dot(t, wg_ref[...]) for t in h]
    u = [_dot(t, wu_ref[...]) for t in h]
    a = [(tg * (1.0 / (1.0 + jnp.exp(-tg))) * tu).astype(BF16) for tg, tu in zip(g, u)]
    d = [_dot(t, wd_ref[...]) for t in a]
    for rows, t in zip(parts, d):
        y_ref[rows, :] = x_ref[rows, :] + _rms(t, npost_ref[...])


def _ffn(x2d, w):
    tokens = x2d.shape[0]
    tm = MIX_ROW_SPLIT * TOKEN_TILE
    row = lambda i: (i, 0)
    nv = _const_spec((1, D_MODEL))
    return pl.pallas_call(
        _ffn_kernel,
        grid=(tokens // tm,),
        in_specs=[pl.BlockSpec((tm, D_MODEL), row), nv,
                  _const_spec((D_MODEL, D_FF)), _const_spec((D_MODEL, D_FF)),
                  _const_spec((D_FF, D_MODEL)), nv],
        out_specs=pl.BlockSpec((tm, D_MODEL), row),
        out_shape=jax.ShapeDtypeStruct((tokens, D_MODEL), F32),
        compiler_params=_params(("parallel",)),
        name="ffn",
    )(x2d, w["norm_ffn_pre"], w["w_ffn_gate"], w["w_ffn_up"], w["w_ffn_down"], w["norm_ffn_post"])


def _prepare_weights(p):
    w_in = p["w_in"]
    gates_at = 2 * GLA_QK + GLA_V
    gates_end = gates_at + 2 * GLA_GATE_RANK
    w = {}
    w["w_a"] = jnp.concatenate([w_in[:, :gates_at], w_in[:, gates_end:]], axis=1).astype(BF16)
    w["w_g"] = jnp.pad(w_in[:, gates_at:gates_end], ((0, 0), (0, LANES - 2 * GLA_GATE_RANK))).astype(BF16)
    w_up = jnp.zeros((LANES, 2 * GLA_QK), F32)
    w_up = w_up.at[:GLA_GATE_RANK, :GLA_QK].set(p["w_gate_up_f"])
    w_up = w_up.at[GLA_GATE_RANK:2 * GLA_GATE_RANK, GLA_QK:].set(p["w_gate_up_b"])
    w["w_up"] = w_up.astype(BF16)
    w["b_g"] = jnp.concatenate([p["b_gate_f"], p["b_gate_b"]])[None, :]
    for name in ("w_out", "w_xq", "w_xkv", "w_xo", "w_ffn_gate", "w_ffn_up", "w_ffn_down"):
        w[name] = p[name].astype(BF16)
    for name in ("norm_mix_pre", "gla_norm_w", "lambda_q1", "lambda_k1", "lambda_q2", "lambda_k2",
                 "diff_subln_w", "norm_mix_post", "norm_xattn_pre", "norm_mem", "norm_xattn_post",
                 "norm_ffn_pre", "norm_ffn_post"):
        w[name] = p[name][None, :]
    return w


def _encoder_layer(x, mem, w, lam_init):
    batch, seq, _ = x.shape
    assert seq % TOKEN_TILE == 0 and GLA_BLOCK == TOKEN_TILE and mem.shape[1] == MEM_LEN
    x2d = x.reshape(batch * seq, D_MODEL)
    mem_k, mem_v = _mem_kv(mem.reshape(batch * MEM_LEN, D_MODEL), w["norm_mem"], w["w_xkv"])
    f, b, v, og, decf, decb, dq, dk, dv = _in_proj(x2d, seq, w)
    o_f, o_b = _gla(f, b, v, decf, decb, batch, seq)
    o_diff = _diff_attn(dq, dk, dv, w, batch, seq, lam_init)
    x2 = _mix_xattn(x2d, o_f, o_b, og, o_diff, mem_k, mem_v, seq, w)
    return _ffn(x2, w).reshape(batch, seq, D_MODEL)


def kernel(x_prompt, x_sample, mem_prompt, mem_sample, norm_mix_pre, w_in, w_gate_up_f, b_gate_f,
           w_gate_up_b, b_gate_b, gla_norm_w, lambda_q1, lambda_k1, lambda_q2, lambda_k2, diff_subln_w,
           w_out, norm_mix_post, norm_xattn_pre, norm_mem, w_xq, w_xkv, w_xo, norm_xattn_post,
           norm_ffn_pre, w_ffn_gate, w_ffn_up, w_ffn_down, norm_ffn_post):
    stacked = dict(
        norm_mix_pre=norm_mix_pre, w_in=w_in, w_gate_up_f=w_gate_up_f, b_gate_f=b_gate_f,
        w_gate_up_b=w_gate_up_b, b_gate_b=b_gate_b, gla_norm_w=gla_norm_w, lambda_q1=lambda_q1,
        lambda_k1=lambda_k1, lambda_q2=lambda_q2, lambda_k2=lambda_k2, diff_subln_w=diff_subln_w,
        w_out=w_out, norm_mix_post=norm_mix_post, norm_xattn_pre=norm_xattn_pre, norm_mem=norm_mem,
        w_xq=w_xq, w_xkv=w_xkv, w_xo=w_xo, norm_xattn_post=norm_xattn_post,
        norm_ffn_pre=norm_ffn_pre, w_ffn_gate=w_ffn_gate, w_ffn_up=w_ffn_up, w_ffn_down=w_ffn_down,
        norm_ffn_post=norm_ffn_post)
    y_prompt, y_sample = x_prompt, x_sample
    for layer in range(w_in.shape[0]):
        lam_init = 0.8 - 0.6 * math.exp(-0.3 * layer)
        w = _prepare_weights({name: t[layer] for name, t in stacked.items()})
        y_prompt = _encoder_layer(y_prompt, mem_prompt, w, lam_init)
        y_sample = _encoder_layer(y_sample, mem_sample, w, lam_init)
    return (y_prompt, y_sample)
```

```python
import functools
import math

import numpy as np
import jax
import jax.numpy as jnp
from jax import lax
from jax.experimental import pallas as pl
from jax.experimental.pallas import tpu as pltpu

F32 = jnp.float32
BF16 = jnp.bfloat16

D_MODEL = 1024
GLA_HEADS = 4
GLA_DK = 64
GLA_DV = 128
GLA_QK = GLA_HEADS * GLA_DK
GLA_V = GLA_HEADS * GLA_DV
GLA_GATE_RANK = 16
GLA_GATE_NORMALIZER = 16.0
GLA_CHUNK = 64
DIFF_HEADS = 4
DIFF_DH = 64
DIFF_QK = DIFF_HEADS * 2 * DIFF_DH
DIFF_V = DIFF_HEADS * 2 * DIFF_DH
ROT_DIM = DIFF_DH // 4
ROPE_THETA = 500000.0
MEM_LEN = 256
XATTN_HEADS = 4
XATTN_DH = D_MODEL // XATTN_HEADS
D_FF = 2816
RMS_EPS = 1e-6

LANES = 128
TOKEN_TILE = 512
GLA_BLOCK = 512
ATTN_Q_TILE = 512
ATTN_K_TILE = 2048
MIX_ROW_SPLIT = 2
ATTN_KEY_BLOCK = 512
ATTN_SCORE_SPLIT = 1
ATTN_ONES_ROWS = 16
LOG2E = math.log2(math.e)
VMEM_LIMIT = 56 * 1024 * 1024

NT_DIMS = (((1,), (1,)), ((), ()))
TN_DIMS = (((0,), (0,)), ((), ()))


def _rms(x, w):
    return x * lax.rsqrt(jnp.mean(x * x, axis=-1, keepdims=True) + RMS_EPS) * w


def _dot(a, b):
    return jnp.dot(a, b, preferred_element_type=F32)


def _dot_nt(a, b):
    return lax.dot_general(a, b, NT_DIMS, preferred_element_type=F32)


def _dot_tn(a, b):
    return lax.dot_general(a, b, TN_DIMS, preferred_element_type=F32)


def _const_spec(shape):
    return pl.BlockSpec(shape, lambda *_: (0,) * len(shape), pipeline_mode=pl.Buffered(1))


def _params(semantics):
    return pltpu.CompilerParams(dimension_semantics=semantics, vmem_limit_bytes=VMEM_LIMIT)


def _mem_kv_kernel(m_ref, nw_ref, w_ref, k_ref, v_ref):
    m = _rms(m_ref[...], nw_ref[...]).astype(BF16)
    kv = _dot(m, w_ref[...])
    k_ref[...] = kv[:, :D_MODEL].astype(BF16)
    v_ref[...] = kv[:, D_MODEL:].astype(BF16)


def _mem_kv(mem2d, norm_w, w_xkv):
    rows = mem2d.shape[0]
    out = jax.ShapeDtypeStruct((rows, D_MODEL), BF16)
    return pl.pallas_call(
        _mem_kv_kernel,
        grid=(rows // MEM_LEN,),
        in_specs=[pl.BlockSpec((MEM_LEN, D_MODEL), lambda i: (i, 0)),
                  _const_spec((1, D_MODEL)),
                  _const_spec((D_MODEL, 2 * D_MODEL))],
        out_specs=[pl.BlockSpec((MEM_LEN, D_MODEL), lambda i: (i, 0))] * 2,
        out_shape=[out, out],
        compiler_params=_params(("parallel",)),
        name="mem_kv",
    )(mem2d, norm_w, w_xkv)


def _inproj_kernel(x_ref, nw_ref, wa_ref, wg_ref, wup_ref, bg_ref, tri_ref, cos_ref, sin_ref,
                   f_ref, b_ref, v_ref, og_ref, decf_ref, decb_ref, dq_ref, dk_ref, dvt_ref):
    tm = x_ref.shape[0]
    h = _rms(x_ref[...], nw_ref[...]).astype(BF16)

    gdown = _dot(h, wg_ref[...]).astype(BF16)
    pre = _dot(gdown, wup_ref[...]) + bg_ref[...]
    g = (jnp.minimum(pre, 0.0) - jnp.log1p(jnp.exp(-jnp.abs(pre)))) * (1.0 / GLA_GATE_NORMALIZER)
    g_hi = g.astype(BF16)
    g_lo = (g - g_hi.astype(F32)).astype(BF16)
    g_split = jnp.concatenate([g_hi, g_lo], axis=1)
    tri = tri_ref[...]
    half = tri.shape[0]
    pre_sum = jnp.concatenate([_dot(tri, g_split[r:r + half]) for r in range(0, tm, half)], axis=0)
    prefix = pre_sum[:, :2 * GLA_QK] + pre_sum[:, 2 * GLA_QK:]
    totals = [prefix[c + GLA_CHUNK - 1:c + GLA_CHUNK] for c in range(0, tm, GLA_CHUNK)]
    total = jnp.concatenate([jnp.broadcast_to(t, (GLA_CHUNK, 2 * GLA_QK)) for t in totals], axis=0)
    dec = jnp.exp(jnp.concatenate(totals, axis=0))
    pf, pb = prefix[:, :GLA_QK], prefix[:, GLA_QK:]
    tf, tb = total[:, :GLA_QK], total[:, GLA_QK:]
    gb = g[:, GLA_QK:]

    qk = _dot(h, wa_ref[:, 0:2 * GLA_QK])
    q = qk[:, :GLA_QK] * (GLA_DK ** -0.5)
    k = qk[:, GLA_QK:]
    for cum, rest, o_ref in ((pf, tf - pf, f_ref), (tb - pb + gb, pb - gb, b_ref)):
        o_ref[:, 0:GLA_QK] = (q * jnp.exp(cum)).astype(BF16)
        o_ref[:, GLA_QK:2 * GLA_QK] = (k * jnp.exp(-cum)).astype(BF16)
        o_ref[:, 2 * GLA_QK:3 * GLA_QK] = (k * jnp.exp(rest)).astype(BF16)
    decf_ref[0] = dec[:, :GLA_QK].T
    decb_ref[0] = dec[:, GLA_QK:].T

    off = 2 * GLA_QK
    v_ref[...] = _dot(h, wa_ref[:, off:off + GLA_V]).astype(BF16)
    off += GLA_V
    og_ref[...] = _dot(h, wa_ref[:, off:off + GLA_V]).astype(BF16)
    off += GLA_V

    cos = cos_ref[...]
    sin = sin_ref[...]
    lane = lax.broadcasted_iota(jnp.int32, (tm, LANES), 1)
    second_half = (lane & (DIFF_DH - 1)) >= (ROT_DIM // 2)

    def rope(t):
        slabs = []
        for s in range(t.shape[1] // LANES):
            ts = t[:, s * LANES:(s + 1) * LANES]
            partner = jnp.where(second_half, pltpu.roll(ts, ROT_DIM // 2, 1),
                                pltpu.roll(ts, LANES - ROT_DIM // 2, 1))
            slabs.append(ts * cos + partner * sin)
        return jnp.concatenate(slabs, axis=1)

    dq = _dot(h, wa_ref[:, off:off + DIFF_QK])
    dq_ref[...] = (rope(dq) * (DIFF_DH ** -0.5 * LOG2E)).astype(BF16)
    off += DIFF_QK
    dk = _dot(h, wa_ref[:, off:off + DIFF_QK])
    dk_ref[...] = rope(dk).astype(BF16)
    off += DIFF_QK
    dvt_ref[...] = _dot(h, wa_ref[:, off:off + DIFF_V]).T.astype(BF16)


def _chunk_prefix_matrix(rows):
    r = np.arange(rows)[:, None]
    c = np.arange(rows)[None, :]
    return jnp.asarray(((r // GLA_CHUNK) == (c // GLA_CHUNK)) & (c <= r), BF16)


def _rope_tables(seq):
    inv = ROPE_THETA ** (-jnp.arange(0, ROT_DIM, 2, dtype=F32) / ROT_DIM)
    ang = jnp.arange(seq).astype(F32)[:, None] * inv[None, :]
    cos, sin = jnp.cos(ang), jnp.sin(ang)
    rest = DIFF_DH - ROT_DIM
    cos64 = jnp.concatenate([cos, cos, jnp.ones((seq, rest), F32)], axis=1)
    sin64 = jnp.concatenate([-sin, sin, jnp.zeros((seq, rest), F32)], axis=1)
    reps = LANES // DIFF_DH
    return jnp.tile(cos64, (1, reps)), jnp.tile(sin64, (1, reps))


def _in_proj(x2d, seq, w):
    tokens = x2d.shape[0]
    tm = TOKEN_TILE
    nchunk = tm // GLA_CHUNK
    tri = _chunk_prefix_matrix(tm // 2)
    cos, sin = _rope_tables(seq)
    pos_blocks = seq // tm
    row = lambda i: (i, 0)
    pos = lambda i: (i % pos_blocks, 0)

    def tok(width, dtype=BF16):
        return jax.ShapeDtypeStruct((tokens, width), dtype)

    dec = jax.ShapeDtypeStruct((tokens // tm, GLA_QK, nchunk), F32)
    dec_spec = pl.BlockSpec((1, GLA_QK, nchunk), lambda i: (i, 0, 0))
    return pl.pallas_call(
        _inproj_kernel,
        grid=(tokens // tm,),
        in_specs=[pl.BlockSpec((tm, D_MODEL), row),
                  _const_spec((1, D_MODEL)),
                  _const_spec(w["w_a"].shape),
                  _const_spec(w["w_g"].shape),
                  _const_spec(w["w_up"].shape),
                  _const_spec(w["b_g"].shape),
                  _const_spec(tri.shape),
                  pl.BlockSpec((tm, LANES), pos),
                  pl.BlockSpec((tm, LANES), pos)],
        out_specs=[pl.BlockSpec((tm, 3 * GLA_QK), row),
                   pl.BlockSpec((tm, 3 * GLA_QK), row),
                   pl.BlockSpec((tm, GLA_V), row),
                   pl.BlockSpec((tm, GLA_V), row),
                   dec_spec,
                   dec_spec,
                   pl.BlockSpec((tm, DIFF_QK), row),
                   pl.BlockSpec((tm, DIFF_QK), row),
                   pl.BlockSpec((DIFF_V, tm), lambda i: (0, i))],
        out_shape=[tok(3 * GLA_QK), tok(3 * GLA_QK), tok(GLA_V), tok(GLA_V), dec, dec,
                   tok(DIFF_QK), tok(DIFF_QK), jax.ShapeDtypeStruct((DIFF_V, tokens), BF16)],
        compiler_params=_params(("parallel",)),
        name="in_proj",
    )(x2d, w["norm_mix_pre"], w["w_a"], w["w_g"], w["w_up"], w["b_g"], tri, cos, sin)


def _gla_kernel(f_ref, b_ref, vf_ref, vb_ref, decf_ref, decb_ref, of_ref, ob_ref, sf_ref, sb_ref):
    nchunk = f_ref.shape[0] // GLA_CHUNK

    @pl.when(pl.program_id(1) == 0)
    def _():
        sf_ref[...] = jnp.zeros_like(sf_ref)
        sb_ref[...] = jnp.zeros_like(sb_ref)

    qk_head = lax.broadcasted_iota(jnp.int32, (GLA_CHUNK, GLA_QK), 1) // GLA_DK
    v_head = lax.broadcasted_iota(jnp.int32, (GLA_CHUNK, GLA_V), 1) // GLA_DV
    s_head = lax.broadcasted_iota(jnp.int32, (GLA_QK, GLA_DV), 0) // GLA_DK
    a_row = lax.broadcasted_iota(jnp.int32, (GLA_CHUNK, GLA_QK), 0)
    a_col = lax.broadcasted_iota(jnp.int32, (GLA_CHUNK, GLA_QK), 1) % GLA_CHUNK

    def chunk_step(x_ref, v_ref, dec_ref, o_ref, s_ref, c, keep):
        rows = slice(c * GLA_CHUNK, (c + 1) * GLA_CHUNK)
        qe = x_ref[rows, 0:GLA_QK]
        ke = x_ref[rows, GLA_QK:2 * GLA_QK]
        kd = x_ref[rows, 2 * GLA_QK:3 * GLA_QK]
        v = v_ref[rows, :]
        dec = dec_ref[0, :, c:c + 1]
        ke_heads = jnp.concatenate([jnp.where(qk_head == h, ke, jnp.zeros_like(ke))
                                    for h in range(GLA_HEADS)], axis=0)
        a = jnp.where(keep, _dot_nt(qe, ke_heads), 0.0).astype(BF16)
        v_heads = jnp.concatenate([jnp.where(v_head == h, v, jnp.zeros_like(v))
                                   for h in range(GLA_HEADS)], axis=0)
        state = s_ref[...]
        state16 = state.astype(BF16)
        s_heads = jnp.concatenate([jnp.where(s_head == h, state16, jnp.zeros_like(state16))
                                   for h in range(GLA_HEADS)], axis=1)
        o_ref[rows, :] = (_dot(a, v_heads) + _dot(qe, s_heads)).astype(BF16)
        upd = jnp.concatenate([_dot_tn(kd[:, h * GLA_DK:(h + 1) * GLA_DK],
                                       v[:, h * GLA_DV:(h + 1) * GLA_DV])
                               for h in range(GLA_HEADS)], axis=0)
        s_ref[...] = dec * state + upd

    for c in range(nchunk):
        chunk_step(f_ref, vf_ref, decf_ref, of_ref, sf_ref, c, a_col <= a_row)
        chunk_step(b_ref, vb_ref, decb_ref, ob_ref, sb_ref, nchunk - 1 - c, a_col >= a_row)


def _gla(f, b, v, decf, decb, batch, seq):
    tokens = f.shape[0]
    lb = GLA_BLOCK
    nblk = seq // lb
    nchunk = lb // GLA_CHUNK
    fwd = lambda bi, i: (bi * nblk + i, 0)
    bwd = lambda bi, i: (bi * nblk + nblk - 1 - i, 0)
    fwd3 = lambda bi, i: (bi * nblk + i, 0, 0)
    bwd3 = lambda bi, i: (bi * nblk + nblk - 1 - i, 0, 0)
    out = jax.ShapeDtypeStruct((tokens, GLA_V), BF16)
    return pl.pallas_call(
        _gla_kernel,
        grid=(batch, nblk),
        in_specs=[pl.BlockSpec((lb, 3 * GLA_QK), fwd),
                  pl.BlockSpec((lb, 3 * GLA_QK), bwd),
                  pl.BlockSpec((lb, GLA_V), fwd),
                  pl.BlockSpec((lb, GLA_V), bwd),
                  pl.BlockSpec((1, GLA_QK, nchunk), fwd3),
                  pl.BlockSpec((1, GLA_QK, nchunk), bwd3)],
        out_specs=[pl.BlockSpec((lb, GLA_V), fwd),
                   pl.BlockSpec((lb, GLA_V), bwd)],
        out_shape=[out, out],
        scratch_shapes=[pltpu.VMEM((GLA_QK, GLA_DV), F32),
                        pltpu.VMEM((GLA_QK, GLA_DV), F32)],
        compiler_params=_params(("parallel", "arbitrary")),
        name="gla",
    )(f, b, v, v, decf, decb)


def _diff_kernel(q_ref, k_ref, vt_ref, lq1_ref, lk1_ref, lq2_ref, lk2_ref, w_ref, o_ref,
                 qm_ref, m_ref, l_ref, acc_ref, s_ref, *, lam_init):
    ki = pl.program_id(2)
    tq = q_ref.shape[0]
    tk = k_ref.shape[0]
    width = 2 * DIFF_DH
    nstream = 2 * DIFF_HEADS

    @pl.when(ki == 0)
    def _():
        m_ref[...] = jnp.full_like(m_ref, -jnp.inf)
        l_ref[...] = jnp.zeros_like(l_ref)
        acc_ref[...] = jnp.zeros_like(acc_ref)
        first = lax.broadcasted_iota(jnp.int32, (tq, width), 1) < DIFF_DH
        for h in range(DIFF_HEADS):
            qh = q_ref[:, h * width:(h + 1) * width]
            zero = jnp.zeros_like(qh)
            qm_ref[2 * h] = jnp.where(first, qh, zero)
            qm_ref[2 * h + 1] = jnp.where(first, zero, qh)

    ones = jnp.ones((ATTN_ONES_ROWS, ATTN_KEY_BLOCK), BF16)

    key_groups = [slice(g, g + tk // ATTN_SCORE_SPLIT) for g in range(0, tk, tk // ATTN_SCORE_SPLIT)]

    def scores(s_idx, keys):
        h = s_idx // 2
        s = _dot_nt(k_ref[keys, h * width:(h + 1) * width], qm_ref[s_idx])
        s_ref[s_idx % 2, keys, :] = s
        return jnp.max(s, axis=0, keepdims=True)

    def new_max(s_idx, group_maxima):
        m_prev = m_ref[s_idx]
        m_next = jnp.maximum(m_prev, functools.reduce(jnp.maximum, group_maxima))
        m_ref[s_idx] = m_next
        return m_prev, m_next

    maxes = new_max(0, [scores(0, keys) for keys in key_groups])
    for s_idx in range(nstream):
        m_prev, m_next = maxes
        h = s_idx // 2
        m_row = m_next[0:1]
        pv = jnp.zeros((width + ATTN_ONES_ROWS, tq), F32)
        group_maxima = []
        for group in key_groups:
            if s_idx + 1 < nstream:
                group_maxima.append(scores(s_idx + 1, group))
            for k0 in range(group.start, group.stop, ATTN_KEY_BLOCK):
                keys = slice(k0, k0 + ATTN_KEY_BLOCK)
                pb = jnp.exp2(s_ref[s_idx % 2, keys, :] - m_row).astype(BF16)
                vt = jnp.concatenate([vt_ref[h * width:(h + 1) * width, keys], ones], axis=0)
                pv = pv + _dot(vt, pb)
        if s_idx + 1 < nstream:
            maxes = new_max(s_idx + 1, group_maxima)
        alpha = jnp.exp2(m_prev - m_next)
        acc_ref[s_idx] = alpha[0:1] * acc_ref[s_idx] + pv[:width]
        l_ref[s_idx] = alpha * l_ref[s_idx] + pv[width:width + 8]

    @pl.when(ki == pl.num_programs(2) - 1)
    def _():
        lam = (jnp.exp(jnp.sum(lq1_ref[...] * lk1_ref[...], axis=1, keepdims=True))
               - jnp.exp(jnp.sum(lq2_ref[...] * lk2_ref[...], axis=1, keepdims=True)) + lam_init)
        for h in range(DIFF_HEADS):
            o1 = acc_ref[2 * h] / l_ref[2 * h][0:1]
            o2 = acc_ref[2 * h + 1] / l_ref[2 * h + 1][0:1]
            o = o1 - lam * o2
            scale = lax.rsqrt(jnp.mean(o * o, axis=0, keepdims=True) + RMS_EPS)
            o = o * scale * w_ref[...] * (1.0 - lam_init)
            o_ref[:, h * width:(h + 1) * width] = o.T.astype(BF16)


def _diff_attn(dq, dk, dvt, w, batch, seq, lam_init):
    tokens = dq.shape[0]
    tq = min(ATTN_Q_TILE, seq)
    tk = min(ATTN_K_TILE, seq)
    nq, nk = seq // tq, seq // tk
    qmap = lambda b, qi, ki: (b * nq + qi, 0)
    kmap = lambda b, qi, ki: (b * nk + ki, 0)
    vmap = lambda b, qi, ki: (0, b * nk + ki)
    nstream = 2 * DIFF_HEADS
    vec = _const_spec((1, DIFF_DH))
    return pl.pallas_call(
        functools.partial(_diff_kernel, lam_init=lam_init),
        grid=(batch, nq, nk),
        in_specs=[pl.BlockSpec((tq, DIFF_QK), qmap),
                  pl.BlockSpec((tk, DIFF_QK), kmap),
                  pl.BlockSpec((DIFF_V, tk), vmap),
                  vec, vec, vec, vec,
                  _const_spec((2 * DIFF_DH, 1))],
        out_specs=pl.BlockSpec((tq, DIFF_V), qmap),
        out_shape=jax.ShapeDtypeStruct((tokens, DIFF_V), BF16),
        scratch_shapes=[pltpu.VMEM((nstream, tq, 2 * DIFF_DH), BF16),
                        pltpu.VMEM((nstream, 8, tq), F32),
                        pltpu.VMEM((nstream, 8, tq), F32),
                        pltpu.VMEM((nstream, 2 * DIFF_DH, tq), F32),
                        pltpu.VMEM((2, tk, tq), F32)],
        compiler_params=_params(("parallel", "parallel", "arbitrary")),
        name="diff_attn",
    )(dq, dk, dvt, w["lambda_q1"], w["lambda_k1"], w["lambda_q2"], w["lambda_k2"],
      w["diff_subln_w"].reshape(2 * DIFF_DH, 1))


def _mix_xattn_kernel(x_ref, of_ref, ob_ref, og_ref, od_ref, k_ref, v_ref,
                      gnw_ref, wout_ref, npost_ref, nxpre_ref, wxq_ref, wxo_ref, nxpost_ref, y_ref):
    tm = x_ref.shape[0]
    parts = [slice(r, r + tm // MIX_ROW_SPLIT) for r in range(0, tm, tm // MIX_ROW_SPLIT)]

    def gla_out(rows):
        o = of_ref[rows, :].astype(F32) + ob_ref[rows, :].astype(F32)
        og = og_ref[rows, :].astype(F32)
        gate = og * (1.0 / (1.0 + jnp.exp(-og)))
        heads = []
        for h in range(GLA_HEADS):
            cols = slice(h * GLA_DV, (h + 1) * GLA_DV)
            heads.append((_rms(o[:, cols], gnw_ref[...]) * gate[:, cols]).astype(BF16))
        return jnp.concatenate(heads + [od_ref[rows, :]], axis=1)

    def attend(q):
        outs = []
        for h in range(XATTN_HEADS):
            cols = slice(h * XATTN_DH, (h + 1) * XATTN_DH)
            s = _dot_nt(q[:, cols], k_ref[:, cols])
            p = jnp.exp(s - jnp.max(s, axis=1, keepdims=True))
            denom = jnp.sum(p, axis=1, keepdims=True)
            outs.append((_dot(p.astype(BF16), v_ref[:, cols]) / denom).astype(BF16))
        return jnp.concatenate(outs, axis=1)

    mix_in = [gla_out(rows) for rows in parts]
    mix = [_dot(t, wout_ref[...]) for t in mix_in]
    x1 = [x_ref[rows, :] + _rms(t, npost_ref[...]) for rows, t in zip(parts, mix)]
    h2 = [_rms(t, nxpre_ref[...]).astype(BF16) for t in x1]
    q = [(_dot(t, wxq_ref[...]) * (XATTN_DH ** -0.5)).astype(BF16) for t in h2]
    att = [attend(t) for t in q]
    xo = [_dot(t, wxo_ref[...]) for t in att]
    for rows, res, t in zip(parts, x1, xo):
        y_ref[rows, :] = res + _rms(t, nxpost_ref[...])


def _mix_xattn(x2d, o_f, o_b, og, o_diff, mem_k, mem_v, seq, w):
    tokens = x2d.shape[0]
    tm = MIX_ROW_SPLIT * TOKEN_TILE
    per_seq = seq // tm
    row = lambda i: (i, 0)
    memmap = lambda i: (i // per_seq, 0)
    half = pl.BlockSpec((tm, GLA_V), row)
    sq = _const_spec((D_MODEL, D_MODEL))
    nv = _const_spec((1, D_MODEL))
    return pl.pallas_call(
        _mix_xattn_kernel,
        grid=(tokens // tm,),
        in_specs=[pl.BlockSpec((tm, D_MODEL), row), half, half, half, half,
                  pl.BlockSpec((MEM_LEN, D_MODEL), memmap),
                  pl.BlockSpec((MEM_LEN, D_MODEL), memmap),
                  _const_spec((1, GLA_DV)), sq, nv, nv, sq, sq, nv],
        out_specs=pl.BlockSpec((tm, D_MODEL), row),
        out_shape=jax.ShapeDtypeStruct((tokens, D_MODEL), F32),
        compiler_params=_params(("parallel",)),
        name="mix_xattn",
    )(x2d, o_f, o_b, og, o_diff, mem_k, mem_v, w["gla_norm_w"], w["w_out"], w["norm_mix_post"],
      w["norm_xattn_pre"], w["w_xq"], w["w_xo"], w["norm_xattn_post"])


def _ffn_kernel(x_ref, npre_ref, wg_ref, wu_ref, wd_ref, npost_ref, y_ref):
    tm = x_ref.shape[0]
    parts = [slice(r, r + tm // MIX_ROW_SPLIT) for r in range(0, tm, tm // MIX_ROW_SPLIT)]
    h = [_rms(x_ref[rows, :], npre_ref[...]).astype(BF16) for rows in parts]
    g = [_dot(t, wg_ref[...]) for t in h]
    u = [_dot(t, wu_ref[...]) for t in h]
    a = [(tg * (1.0 / (1.0 + jnp.exp(-tg))) * tu).astype(BF16) for tg, tu in zip(g, u)]
    d = [_dot(t, wd_ref[...]) for t in a]
    for rows, t in zip(parts, d):
        y_ref[rows, :] = x_ref[rows, :] + _rms(t, npost_ref[...])


def _ffn(x2d, w):
    tokens = x2d.shape[0]
    tm = MIX_ROW_SPLIT * TOKEN_TILE
    row = lambda i: (i, 0)
    nv = _const_spec((1, D_MODEL))
    return pl.pallas_call(
        _ffn_kernel,
        grid=(tokens // tm,),
        in_specs=[pl.BlockSpec((tm, D_MODEL), row), nv,
                  _const_spec((D_MODEL, D_FF)), _const_spec((D_MODEL, D_FF)),
                  _const_spec((D_FF, D_MODEL)), nv],
        out_specs=pl.BlockSpec((tm, D_MODEL), row),
        out_shape=jax.ShapeDtypeStruct((tokens, D_MODEL), F32),
        compiler_params=_params(("parallel",)),
        name="ffn",
    )(x2d, w["norm_ffn_pre"], w["w_ffn_gate"], w["w_ffn_up"], w["w_ffn_down"], w["norm_ffn_post"])


def _prepare_weights(p):
    w_in = p["w_in"]
    gates_at = 2 * GLA_QK + GLA_V
    gates_end = gates_at + 2 * GLA_GATE_RANK
    w = {}
    w["w_a"] = jnp.concatenate([w_in[:, :gates_at], w_in[:, gates_end:]], axis=1).astype(BF16)
    w["w_g"] = jnp.pad(w_in[:, gates_at:gates_end], ((0, 0), (0, LANES - 2 * GLA_GATE_RANK))).astype(BF16)
    w_up = jnp.zeros((LANES, 2 * GLA_QK), F32)
    w_up = w_up.at[:GLA_GATE_RANK, :GLA_QK].set(p["w_gate_up_f"])
    w_up = w_up.at[GLA_GATE_RANK:2 * GLA_GATE_RANK, GLA_QK:].set(p["w_gate_up_b"])
    w["w_up"] = w_up.astype(BF16)
    w["b_g"] = jnp.concatenate([p["b_gate_f"], p["b_gate_b"]])[None, :]
    for name in ("w_out", "w_xq", "w_xkv", "w_xo", "w_ffn_gate", "w_ffn_up", "w_ffn_down"):
        w[name] = p[name].astype(BF16)
    for name in ("norm_mix_pre", "gla_norm_w", "lambda_q1", "lambda_k1", "lambda_q2", "lambda_k2",
                 "diff_subln_w", "norm_mix_post", "norm_xattn_pre", "norm_mem", "norm_xattn_post",
                 "norm_ffn_pre", "norm_ffn_post"):
        w[name] = p[name][None, :]
    return w


def _encoder_layer(x, mem, w, lam_init):
    batch, seq, _ = x.shape
    assert seq % TOKEN_TILE == 0 and GLA_BLOCK == TOKEN_TILE and mem.shape[1] == MEM_LEN
    x2d = x.reshape(batch * seq, D_MODEL)
    mem_k, mem_v = _mem_kv(mem.reshape(batch * MEM_LEN, D_MODEL), w["norm_mem"], w["w_xkv"])
    f, b, v, og, decf, decb, dq, dk, dvt = _in_proj(x2d, seq, w)
    o_f, o_b = _gla(f, b, v, decf, decb, batch, seq)
    o_diff = _diff_attn(dq, dk, dvt, w, batch, seq, lam_init)
    x2 = _mix_xattn(x2d, o_f, o_b, og, o_diff, mem_k, mem_v, seq, w)
    return _ffn(x2, w).reshape(batch, seq, D_MODEL)


def kernel(x_prompt, x_sample, mem_prompt, mem_sample, norm_mix_pre, w_in, w_gate_up_f, b_gate_f,
           w_gate_up_b, b_gate_b, gla_norm_w, lambda_q1, lambda_k1, lambda_q2, lambda_k2, diff_subln_w,
           w_out, norm_mix_post, norm_xattn_pre, norm_mem, w_xq, w_xkv, w_xo, norm_xattn_post,
           norm_ffn_pre, w_ffn_gate, w_ffn_up, w_ffn_down, norm_ffn_post):
    stacked = dict(
        norm_mix_pre=norm_mix_pre, w_in=w_in, w_gate_up_f=w_gate_up_f, b_gate_f=b_gate_f,
        w_gate_up_b=w_gate_up_b, b_gate_b=b_gate_b, gla_norm_w=gla_norm_w, lambda_q1=lambda_q1,
        lambda_k1=lambda_k1, lambda_q2=lambda_q2, lambda_k2=lambda_k2, diff_subln_w=diff_subln_w,
        w_out=w_out, norm_mix_post=norm_mix_post, norm_xattn_pre=norm_xattn_pre, norm_mem=norm_mem,
        w_xq=w_xq, w_xkv=w_xkv, w_xo=w_xo, norm_xattn_post=norm_xattn_post,
        norm_ffn_pre=norm_ffn_pre, w_ffn_gate=w_ffn_gate, w_ffn_up=w_ffn_up, w_ffn_down=w_ffn_down,
        norm_ffn_post=norm_ffn_post)
    y_prompt, y_sample = x_prompt, x_sample
    for layer in range(w_in.shape[0]):
        lam_init = 0.8 - 0.6 * math.exp(-0.3 * layer)
        w = _prepare_weights({name: t[layer] for name, t in stacked.items()})
        y_prompt = _encoder_layer(y_prompt, mem_prompt, w, lam_init)
        y_sample = _encoder_layer(y_sample, mem_sample, w, lam_init)
    return (y_prompt, y_sample)
```

```python
import functools
import math

import numpy as np
import jax
import jax.numpy as jnp
from jax import lax
from jax.experimental import pallas as pl
from jax.experimental.pallas import tpu as pltpu

F32 = jnp.float32
BF16 = jnp.bfloat16

D_MODEL = 1024
GLA_HEADS = 4
GLA_DK = 64
GLA_DV = 128
GLA_QK = GLA_HEADS * GLA_DK
GLA_V = GLA_HEADS * GLA_DV
GLA_GATE_RANK = 16
GLA_GATE_NORMALIZER = 16.0
GLA_CHUNK = 64
DIFF_HEADS = 4
DIFF_DH = 64
DIFF_QK = DIFF_HEADS * 2 * DIFF_DH
DIFF_V = DIFF_HEADS * 2 * DIFF_DH
ROT_DIM = DIFF_DH // 4
ROPE_THETA = 500000.0
MEM_LEN = 256
XATTN_HEADS = 4
XATTN_DH = D_MODEL // XATTN_HEADS
D_FF = 2816
RMS_EPS = 1e-6

LANES = 128
TOKEN_TILE = 512
GLA_BLOCK = 512
INPROJ_TILE = 1024
PREFIX_ROWS = 256
ATTN_Q_TILE = 512
ATTN_K_TILE = 2048
MIX_ROW_SPLIT = 2
ATTN_KEY_BLOCK = 256
LOG2E = math.log2(math.e)
VMEM_LIMIT = 56 * 1024 * 1024

NT_DIMS = (((1,), (1,)), ((), ()))
TN_DIMS = (((0,), (0,)), ((), ()))


def _rms(x, w):
    return x * lax.rsqrt(jnp.mean(x * x, axis=-1, keepdims=True) + RMS_EPS) * w


def _dot(a, b):
    return jnp.dot(a, b, preferred_element_type=F32)


def _dot_nt(a, b):
    return lax.dot_general(a, b, NT_DIMS, preferred_element_type=F32)


def _dot_tn(a, b):
    return lax.dot_general(a, b, TN_DIMS, preferred_element_type=F32)


def _const_spec(shape):
    return pl.BlockSpec(shape, lambda *_: (0,) * len(shape), pipeline_mode=pl.Buffered(1))


def _params(semantics):
    return pltpu.CompilerParams(dimension_semantics=semantics, vmem_limit_bytes=VMEM_LIMIT)


def _mem_kv_kernel(m_ref, nw_ref, w_ref, k_ref, v_ref):
    m = _rms(m_ref[...], nw_ref[...]).astype(BF16)
    kv = _dot(m, w_ref[...])
    k_ref[...] = kv[:, :D_MODEL].astype(BF16)
    v_ref[...] = kv[:, D_MODEL:].astype(BF16)


def _mem_kv(mem2d, norm_w, w_xkv):
    rows = mem2d.shape[0]
    out = jax.ShapeDtypeStruct((rows, D_MODEL), BF16)
    return pl.pallas_call(
        _mem_kv_kernel,
        grid=(rows // MEM_LEN,),
        in_specs=[pl.BlockSpec((MEM_LEN, D_MODEL), lambda i: (i, 0)),
                  _const_spec((1, D_MODEL)),
                  _const_spec((D_MODEL, 2 * D_MODEL))],
        out_specs=[pl.BlockSpec((MEM_LEN, D_MODEL), lambda i: (i, 0))] * 2,
        out_shape=[out, out],
        compiler_params=_params(("parallel",)),
        name="mem_kv",
    )(mem2d, norm_w, w_xkv)


def _inproj_kernel(x_ref, nw_ref, wa_ref, wg_ref, wup_ref, bg_ref, tri_ref, cos_ref, sin_ref,
                   f_ref, b_ref, v_ref, og_ref, decf_ref, decb_ref, dq_ref, dk_ref, dv_ref):
    for j in range(x_ref.shape[0] // GLA_BLOCK):
        rows = pl.ds(j * GLA_BLOCK, GLA_BLOCK)
        blk = pl.ds(j, 1)
        _inproj_rows(x_ref.at[rows], nw_ref, wa_ref, wg_ref, wup_ref, bg_ref, tri_ref,
                     cos_ref.at[rows], sin_ref.at[rows], f_ref.at[rows], b_ref.at[rows],
                     v_ref.at[rows], og_ref.at[rows], decf_ref.at[blk], decb_ref.at[blk],
                     dq_ref.at[rows], dk_ref.at[rows], dv_ref.at[rows])


def _inproj_rows(x_ref, nw_ref, wa_ref, wg_ref, wup_ref, bg_ref, tri_ref, cos_ref, sin_ref,
                 f_ref, b_ref, v_ref, og_ref, decf_ref, decb_ref, dq_ref, dk_ref, dv_ref):
    tm = x_ref.shape[0]
    h = _rms(x_ref[...], nw_ref[...]).astype(BF16)

    def proj(off, width):
        return _dot(h, wa_ref[:, off:off + width])

    gdown = _dot(h, wg_ref[...]).astype(BF16)
    pre = _dot(gdown, wup_ref[...]) + bg_ref[...]

    off = 2 * GLA_QK
    v_ref[...] = proj(off, GLA_V).astype(BF16)
    off += GLA_V
    og_ref[...] = proj(off, GLA_V).astype(BF16)
    off += GLA_V

    cos = cos_ref[...]
    sin = sin_ref[...]
    lane = lax.broadcasted_iota(jnp.int32, (tm, LANES), 1)
    second_half = (lane & (DIFF_DH - 1)) >= (ROT_DIM // 2)

    def rope(t):
        slabs = []
        for s in range(t.shape[1] // LANES):
            ts = t[:, s * LANES:(s + 1) * LANES]
            partner = jnp.where(second_half, pltpu.roll(ts, ROT_DIM // 2, 1),
                                pltpu.roll(ts, LANES - ROT_DIM // 2, 1))
            slabs.append(ts * cos + partner * sin)
        return jnp.concatenate(slabs, axis=1)

    dq = proj(off, DIFF_QK)
    dq_ref[...] = (rope(dq) * (DIFF_DH ** -0.5 * LOG2E)).astype(BF16)
    off += DIFF_QK
    dk = proj(off, DIFF_QK)
    dk_ref[...] = rope(dk).astype(BF16)
    off += DIFF_QK
    dv_ref[...] = proj(off, DIFF_V).astype(BF16)

    g = (jnp.minimum(pre, 0.0) - jnp.log1p(jnp.exp(-jnp.abs(pre)))) * (1.0 / GLA_GATE_NORMALIZER)
    g_hi = g.astype(BF16)
    g_lo = (g - g_hi.astype(F32)).astype(BF16)
    g_split = jnp.concatenate([g_hi, g_lo], axis=1)
    tri = tri_ref[...]
    half = tri.shape[0]
    pre_sum = jnp.concatenate([_dot(tri, g_split[r:r + half]) for r in range(0, tm, half)], axis=0)
    prefix = pre_sum[:, :2 * GLA_QK] + pre_sum[:, 2 * GLA_QK:]
    totals = [prefix[c + GLA_CHUNK - 1:c + GLA_CHUNK] for c in range(0, tm, GLA_CHUNK)]
    total = jnp.concatenate([jnp.broadcast_to(t, (GLA_CHUNK, 2 * GLA_QK)) for t in totals], axis=0)
    dec = jnp.exp(jnp.concatenate(totals, axis=0))
    pf, pb = prefix[:, :GLA_QK], prefix[:, GLA_QK:]
    tf, tb = total[:, :GLA_QK], total[:, GLA_QK:]
    gb = g[:, GLA_QK:]

    qk = proj(0, 2 * GLA_QK)
    q = qk[:, :GLA_QK] * (GLA_DK ** -0.5)
    k = qk[:, GLA_QK:]
    for cum, rest, o_ref in ((pf, tf - pf, f_ref), (tb - pb + gb, pb - gb, b_ref)):
        o_ref[:, 0:GLA_QK] = (q * jnp.exp(cum)).astype(BF16)
        o_ref[:, GLA_QK:2 * GLA_QK] = (k * jnp.exp(-cum)).astype(BF16)
        o_ref[:, 2 * GLA_QK:3 * GLA_QK] = (k * jnp.exp(rest)).astype(BF16)
    decf_ref[0] = dec[:, :GLA_QK].T
    decb_ref[0] = dec[:, GLA_QK:].T


def _chunk_prefix_matrix(rows):
    r = np.arange(rows)[:, None]
    c = np.arange(rows)[None, :]
    return jnp.asarray(((r // GLA_CHUNK) == (c // GLA_CHUNK)) & (c <= r), BF16)


def _rope_tables(seq):
    inv = ROPE_THETA ** (-jnp.arange(0, ROT_DIM, 2, dtype=F32) / ROT_DIM)
    ang = jnp.arange(seq).astype(F32)[:, None] * inv[None, :]
    cos, sin = jnp.cos(ang), jnp.sin(ang)
    rest = DIFF_DH - ROT_DIM
    cos64 = jnp.concatenate([cos, cos, jnp.ones((seq, rest), F32)], axis=1)
    sin64 = jnp.concatenate([-sin, sin, jnp.zeros((seq, rest), F32)], axis=1)
    reps = LANES // DIFF_DH
    return jnp.tile(cos64, (1, reps)), jnp.tile(sin64, (1, reps))


def _in_proj(x2d, seq, w):
    tokens = x2d.shape[0]
    tm = INPROJ_TILE
    nchunk = GLA_BLOCK // GLA_CHUNK
    tri = _chunk_prefix_matrix(PREFIX_ROWS)
    cos, sin = _rope_tables(seq)
    pos_blocks = seq // tm
    row = lambda i: (i, 0)
    pos = lambda i: (i % pos_blocks, 0)

    def tok(width, dtype=BF16):
        return jax.ShapeDtypeStruct((tokens, width), dtype)

    dec = jax.ShapeDtypeStruct((tokens // GLA_BLOCK, GLA_QK, nchunk), F32)
    dec_spec = pl.BlockSpec((tm // GLA_BLOCK, GLA_QK, nchunk), lambda i: (i, 0, 0))
    return pl.pallas_call(
        _inproj_kernel,
        grid=(tokens // tm,),
        in_specs=[pl.BlockSpec((tm, D_MODEL), row),
                  _const_spec((1, D_MODEL)),
                  _const_spec(w["w_a"].shape),
                  _const_spec(w["w_g"].shape),
                  _const_spec(w["w_up"].shape),
                  _const_spec(w["b_g"].shape),
                  _const_spec(tri.shape),
                  pl.BlockSpec((tm, LANES), pos),
                  pl.BlockSpec((tm, LANES), pos)],
        out_specs=[pl.BlockSpec((tm, 3 * GLA_QK), row),
                   pl.BlockSpec((tm, 3 * GLA_QK), row),
                   pl.BlockSpec((tm, GLA_V), row),
                   pl.BlockSpec((tm, GLA_V), row),
                   dec_spec,
                   dec_spec,
                   pl.BlockSpec((tm, DIFF_QK), row),
                   pl.BlockSpec((tm, DIFF_QK), row),
                   pl.BlockSpec((tm, DIFF_V), row)],
        out_shape=[tok(3 * GLA_QK), tok(3 * GLA_QK), tok(GLA_V), tok(GLA_V), dec, dec,
                   tok(DIFF_QK), tok(DIFF_QK), tok(DIFF_V)],
        compiler_params=_params(("parallel",)),
        name="in_proj",
    )(x2d, w["norm_mix_pre"], w["w_a"], w["w_g"], w["w_up"], w["b_g"], tri, cos, sin)


def _gla_kernel(f_ref, b_ref, vf_ref, vb_ref, decf_ref, decb_ref, of_ref, ob_ref, sf_ref, sb_ref):
    nchunk = f_ref.shape[0] // GLA_CHUNK

    @pl.when(pl.program_id(1) == 0)
    def _():
        sf_ref[...] = jnp.zeros_like(sf_ref)
        sb_ref[...] = jnp.zeros_like(sb_ref)

    qk_head = lax.broadcasted_iota(jnp.int32, (GLA_CHUNK, GLA_QK), 1) // GLA_DK
    v_head = lax.broadcasted_iota(jnp.int32, (GLA_CHUNK, GLA_V), 1) // GLA_DV
    s_head = lax.broadcasted_iota(jnp.int32, (GLA_QK, GLA_DV), 0) // GLA_DK
    a_row = lax.broadcasted_iota(jnp.int32, (GLA_CHUNK, GLA_QK), 0)
    a_col = lax.broadcasted_iota(jnp.int32, (GLA_CHUNK, GLA_QK), 1) % GLA_CHUNK

    def chunk_step(x_ref, v_ref, dec_ref, o_ref, s_ref, c, keep):
        rows = slice(c * GLA_CHUNK, (c + 1) * GLA_CHUNK)
        qe = x_ref[rows, 0:GLA_QK]
        ke = x_ref[rows, GLA_QK:2 * GLA_QK]
        kd = x_ref[rows, 2 * GLA_QK:3 * GLA_QK]
        v = v_ref[rows, :]
        dec = dec_ref[0, :, c:c + 1]
        ke_heads = jnp.concatenate([jnp.where(qk_head == h, ke, jnp.zeros_like(ke))
                                    for h in range(GLA_HEADS)], axis=0)
        a = jnp.where(keep, _dot_nt(qe, ke_heads), 0.0).astype(BF16)
        v_heads = jnp.concatenate([jnp.where(v_head == h, v, jnp.zeros_like(v))
                                   for h in range(GLA_HEADS)], axis=0)
        state = s_ref[...]
        state16 = state.astype(BF16)
        s_heads = jnp.concatenate([jnp.where(s_head == h, state16, jnp.zeros_like(state16))
                                   for h in range(GLA_HEADS)], axis=1)
        o_ref[rows, :] = (_dot(a, v_heads) + _dot(qe, s_heads)).astype(BF16)
        upd = jnp.concatenate([_dot_tn(kd[:, h * GLA_DK:(h + 1) * GLA_DK],
                                       v[:, h * GLA_DV:(h + 1) * GLA_DV])
                               for h in range(GLA_HEADS)], axis=0)
        s_ref[...] = dec * state + upd

    for c in range(nchunk):
        chunk_step(f_ref, vf_ref, decf_ref, of_ref, sf_ref, c, a_col <= a_row)
        chunk_step(b_ref, vb_ref, decb_ref, ob_ref, sb_ref, nchunk - 1 - c, a_col >= a_row)


def _gla(f, b, v, decf, decb, batch, seq):
    tokens = f.shape[0]
    lb = GLA_BLOCK
    nblk = seq // lb
    nchunk = lb // GLA_CHUNK
    fwd = lambda bi, i: (bi * nblk + i, 0)
    bwd = lambda bi, i: (bi * nblk + nblk - 1 - i, 0)
    fwd3 = lambda bi, i: (bi * nblk + i, 0, 0)
    bwd3 = lambda bi, i: (bi * nblk + nblk - 1 - i, 0, 0)
    out = jax.ShapeDtypeStruct((tokens, GLA_V), BF16)
    return pl.pallas_call(
        _gla_kernel,
        grid=(batch, nblk),
        in_specs=[pl.BlockSpec((lb, 3 * GLA_QK), fwd),
                  pl.BlockSpec((lb, 3 * GLA_QK), bwd),
                  pl.BlockSpec((lb, GLA_V), fwd),
                  pl.BlockSpec((lb, GLA_V), bwd),
                  pl.BlockSpec((1, GLA_QK, nchunk), fwd3),
                  pl.BlockSpec((1, GLA_QK, nchunk), bwd3)],
        out_specs=[pl.BlockSpec((lb, GLA_V), fwd),
                   pl.BlockSpec((lb, GLA_V), bwd)],
        out_shape=[out, out],
        scratch_shapes=[pltpu.VMEM((GLA_QK, GLA_DV), F32),
                        pltpu.VMEM((GLA_QK, GLA_DV), F32)],
        compiler_params=_params(("parallel", "arbitrary")),
        name="gla",
    )(f, b, v, v, decf, decb)


def _diff_kernel(q_ref, k_ref, v_ref, lq1_ref, lk1_ref, lq2_ref, lk2_ref, w_ref, o_ref,
                 qm_ref, m_ref, l_ref, acc_ref, s_ref, *, lam_init):
    ki = pl.program_id(2)
    tq = q_ref.shape[0]
    tk = k_ref.shape[0]
    width = 2 * DIFF_DH

    @pl.when(ki == 0)
    def _():
        m_ref[...] = jnp.full_like(m_ref, -jnp.inf)
        l_ref[...] = jnp.zeros_like(l_ref)
        acc_ref[...] = jnp.zeros_like(acc_ref)
        first = lax.broadcasted_iota(jnp.int32, (tq, width), 1) < DIFF_DH
        for h in range(DIFF_HEADS):
            qh = q_ref[:, h * width:(h + 1) * width]
            zero = jnp.zeros_like(qh)
            qm_ref[2 * h] = jnp.where(first, qh, zero)
            qm_ref[2 * h + 1] = jnp.where(first, zero, qh)

    ones = jnp.ones((ATTN_KEY_BLOCK, width), BF16)
    nstream = 2 * DIFF_HEADS

    def scores(s_idx):
        h = s_idx // 2
        s = _dot_nt(qm_ref[s_idx], k_ref[:, h * width:(h + 1) * width])
        s_ref[s_idx % 2] = s
        m_prev = m_ref[s_idx]
        m_next = jnp.maximum(m_prev, jnp.max(s, axis=1, keepdims=True))
        m_ref[s_idx] = m_next
        return m_prev, m_next

    maxes = scores(0)
    for s_idx in range(nstream):
        m_prev, m_next = maxes
        if s_idx + 1 < nstream:
            maxes = scores(s_idx + 1)
        h = s_idx // 2
        vh = v_ref[:, h * width:(h + 1) * width]
        m_wide = jnp.tile(m_next, (1, ATTN_KEY_BLOCK // LANES))
        pv = jnp.zeros((tq, 2 * width), F32)
        for k0 in range(0, tk, ATTN_KEY_BLOCK):
            keys = slice(k0, k0 + ATTN_KEY_BLOCK)
            pb = jnp.exp2(s_ref[s_idx % 2, :, keys] - m_wide).astype(BF16)
            pv = pv + _dot(pb, jnp.concatenate([vh[keys], ones], axis=1))
        alpha = jnp.exp2(m_prev - m_next)
        acc_ref[s_idx] = alpha * acc_ref[s_idx] + pv[:, :width]
        l_ref[s_idx] = alpha * l_ref[s_idx] + pv[:, width:]

    @pl.when(ki == pl.num_programs(2) - 1)
    def _():
        lam = (jnp.exp(jnp.sum(lq1_ref[...] * lk1_ref[...], axis=1, keepdims=True))
               - jnp.exp(jnp.sum(lq2_ref[...] * lk2_ref[...], axis=1, keepdims=True)) + lam_init)
        for h in range(DIFF_HEADS):
            o1 = acc_ref[2 * h] / l_ref[2 * h]
            o2 = acc_ref[2 * h + 1] / l_ref[2 * h + 1]
            o = _rms(o1 - lam * o2, w_ref[...]) * (1.0 - lam_init)
            o_ref[:, h * width:(h + 1) * width] = o.astype(BF16)


def _diff_attn(dq, dk, dv, w, batch, seq, lam_init):
    tokens = dq.shape[0]
    tq = min(ATTN_Q_TILE, seq)
    tk = min(ATTN_K_TILE, seq)
    nq, nk = seq // tq, seq // tk
    qmap = lambda b, qi, ki: (b * nq + qi, 0)
    kmap = lambda b, qi, ki: (b * nk + ki, 0)
    nstream = 2 * DIFF_HEADS
    vec = _const_spec((1, DIFF_DH))
    return pl.pallas_call(
        functools.partial(_diff_kernel, lam_init=lam_init),
        grid=(batch, nq, nk),
        in_specs=[pl.BlockSpec((tq, DIFF_QK), qmap),
                  pl.BlockSpec((tk, DIFF_QK), kmap),
                  pl.BlockSpec((tk, DIFF_V), kmap),
                  vec, vec, vec, vec,
                  _const_spec((1, 2 * DIFF_DH))],
        out_specs=pl.BlockSpec((tq, DIFF_V), qmap),
        out_shape=jax.ShapeDtypeStruct((tokens, DIFF_V), BF16),
        scratch_shapes=[pltpu.VMEM((nstream, tq, 2 * DIFF_DH), BF16),
                        pltpu.VMEM((nstream, tq, LANES), F32),
                        pltpu.VMEM((nstream, tq, LANES), F32),
                        pltpu.VMEM((nstream, tq, 2 * DIFF_DH), F32),
                        pltpu.VMEM((2, tq, tk), F32)],
        compiler_params=_params(("parallel", "parallel", "arbitrary")),
        name="diff_attn",
    )(dq, dk, dv, w["lambda_q1"], w["lambda_k1"], w["lambda_q2"], w["lambda_k2"], w["diff_subln_w"])


def _mix_xattn_kernel(x_ref, of_ref, ob_ref, og_ref, od_ref, k_ref, v_ref,
                      gnw_ref, wout_ref, npost_ref, nxpre_ref, wxq_ref, wxo_ref, nxpost_ref, y_ref):
    tm = x_ref.shape[0]
    parts = [slice(r, r + tm // MIX_ROW_SPLIT) for r in range(0, tm, tm // MIX_ROW_SPLIT)]

    def gla_out(rows):
        o = of_ref[rows, :].astype(F32) + ob_ref[rows, :].astype(F32)
        og = og_ref[rows, :].astype(F32)
        gate = og * (1.0 / (1.0 + jnp.exp(-og)))
        heads = []
        for h in range(GLA_HEADS):
            cols = slice(h * GLA_DV, (h + 1) * GLA_DV)
            heads.append((_rms(o[:, cols], gnw_ref[...]) * gate[:, cols]).astype(BF16))
        return jnp.concatenate(heads + [od_ref[rows, :]], axis=1)

    def attend(q):
        outs = []
        for h in range(XATTN_HEADS):
            cols = slice(h * XATTN_DH, (h + 1) * XATTN_DH)
            s = _dot_nt(q[:, cols], k_ref[:, cols])
            p = jnp.exp(s - jnp.max(s, axis=1, keepdims=True))
            denom = jnp.sum(p, axis=1, keepdims=True)
            outs.append((_dot(p.astype(BF16), v_ref[:, cols]) / denom).astype(BF16))
        return jnp.concatenate(outs, axis=1)

    mix_in = [gla_out(rows) for rows in parts]
    mix = [_dot(t, wout_ref[...]) for t in mix_in]
    x1 = [x_ref[rows, :] + _rms(t, npost_ref[...]) for rows, t in zip(parts, mix)]
    h2 = [_rms(t, nxpre_ref[...]).astype(BF16) for t in x1]
    q = [(_dot(t, wxq_ref[...]) * (XATTN_DH ** -0.5)).astype(BF16) for t in h2]
    att = [attend(t) for t in q]
    xo = [_dot(t, wxo_ref[...]) for t in att]
    for rows, res, t in zip(parts, x1, xo):
        y_ref[rows, :] = res + _rms(t, nxpost_ref[...])


def _mix_xattn(x2d, o_f, o_b, og, o_diff, mem_k, mem_v, seq, w):
    tokens = x2d.shape[0]
    tm = MIX_ROW_SPLIT * TOKEN_TILE
    per_seq = seq // tm
    row = lambda i: (i, 0)
    memmap = lambda i: (i // per_seq, 0)
    half = pl.BlockSpec((tm, GLA_V), row)
    sq = _const_spec((D_MODEL, D_MODEL))
    nv = _const_spec((1, D_MODEL))
    return pl.pallas_call(
        _mix_xattn_kernel,
        grid=(tokens // tm,),
        in_specs=[pl.BlockSpec((tm, D_MODEL), row), half, half, half, half,
                  pl.BlockSpec((MEM_LEN, D_MODEL), memmap),
                  pl.BlockSpec((MEM_LEN, D_MODEL), memmap),
                  _const_spec((1, GLA_DV)), sq, nv, nv, sq, sq, nv],
        out_specs=pl.BlockSpec((tm, D_MODEL), row),
        out_shape=jax.ShapeDtypeStruct((tokens, D_MODEL), F32),
        compiler_params=_params(("parallel",)),
        name="mix_xattn",
    )(x2d, o_f, o_b, og, o_diff, mem_k, mem_v, w["gla_norm_w"], w["w_out"], w["norm_mix_post"],
      w["norm_xattn_pre"], w["w_xq"], w["w_xo"], w["norm_xattn_post"])


def _ffn_kernel(x_ref, npre_ref, wg_ref, wu_ref, wd_ref, npost_ref, y_ref):
    tm = x_ref.shape[0]
    parts = [slice(r, r + tm // MIX_ROW_SPLIT) for r in range(0, tm, tm // MIX_ROW_SPLIT)]
    h = [_rms(x_ref[rows, :], npre_ref[...]).astype(BF16) for rows in parts]
    g = [_dot(t, wg_ref[...]) for t in h]
    u = [_dot(t, wu_ref[...]) for t in h]
    a = [(tg * (1.0 / (1.0 + jnp.exp(-tg))) * tu).astype(BF16) for tg, tu in zip(g, u)]
    d = [_dot(t, wd_ref[...]) for t in a]
    for rows, t in zip(parts, d):
        y_ref[rows, :] = x_ref[rows, :] + _rms(t, npost_ref[...])


def _ffn(x2d, w):
    tokens = x2d.shape[0]
    tm = MIX_ROW_SPLIT * TOKEN_TILE
    row = lambda i: (i, 0)
    nv = _const_spec((1, D_MODEL))
    return pl.pallas_call(
        _ffn_kernel,
        grid=(tokens // tm,),
        in_specs=[pl.BlockSpec((tm, D_MODEL), row), nv,
                  _const_spec((D_MODEL, D_FF)), _const_spec((D_MODEL, D_FF)),
                  _const_spec((D_FF, D_MODEL)), nv],
        out_specs=pl.BlockSpec((tm, D_MODEL), row),
        out_shape=jax.ShapeDtypeStruct((tokens, D_MODEL), F32),
        compiler_params=_params(("parallel",)),
        name="ffn",
    )(x2d, w["norm_ffn_pre"], w["w_ffn_gate"], w["w_ffn_up"], w["w_ffn_down"], w["norm_ffn_post"])


def _prepare_weights(p):
    w_in = p["w_in"]
    gates_at = 2 * GLA_QK + GLA_V
    gates_end = gates_at + 2 * GLA_GATE_RANK
    w = {}
    w["w_a"] = jnp.concatenate([w_in[:, :gates_at], w_in[:, gates_end:]], axis=1).astype(BF16)
    w["w_g"] = jnp.pad(w_in[:, gates_at:gates_end], ((0, 0), (0, LANES - 2 * GLA_GATE_RANK))).astype(BF16)
    w_up = jnp.zeros((LANES, 2 * GLA_QK), F32)
    w_up = w_up.at[:GLA_GATE_RANK, :GLA_QK].set(p["w_gate_up_f"])
    w_up = w_up.at[GLA_GATE_RANK:2 * GLA_GATE_RANK, GLA_QK:].set(p["w_gate_up_b"])
    w["w_up"] = w_up.astype(BF16)
    w["b_g"] = jnp.concatenate([p["b_gate_f"], p["b_gate_b"]])[None, :]
    for name in ("w_out", "w_xq", "w_xkv", "w_xo", "w_ffn_gate", "w_ffn_up", "w_ffn_down"):
        w[name] = p[name].astype(BF16)
    for name in ("norm_mix_pre", "gla_norm_w", "lambda_q1", "lambda_k1", "lambda_q2", "lambda_k2",
                 "diff_subln_w", "norm_mix_post", "norm_xattn_pre", "norm_mem", "norm_xattn_post",
                 "norm_ffn_pre", "norm_ffn_post"):
        w[name] = p[name][None, :]
    return w


def _encoder_layer(x, mem, w, lam_init):
    batch, seq, _ = x.shape
    assert seq % (MIX_ROW_SPLIT * TOKEN_TILE) == 0 and seq % INPROJ_TILE == 0 and mem.shape[1] == MEM_LEN
    x2d = x.reshape(batch * seq, D_MODEL)
    mem_k, mem_v = _mem_kv(mem.reshape(batch * MEM_LEN, D_MODEL), w["norm_mem"], w["w_xkv"])
    f, b, v, og, decf, decb, dq, dk, dv = _in_proj(x2d, seq, w)
    o_f, o_b = _gla(f, b, v, decf, decb, batch, seq)
    o_diff = _diff_attn(dq, dk, dv, w, batch, seq, lam_init)
    x2 = _mix_xattn(x2d, o_f, o_b, og, o_diff, mem_k, mem_v, seq, w)
    return _ffn(x2, w).reshape(batch, seq, D_MODEL)


def kernel(x_prompt, x_sample, mem_prompt, mem_sample, norm_mix_pre, w_in, w_gate_up_f, b_gate_f,
           w_gate_up_b, b_gate_b, gla_norm_w, lambda_q1, lambda_k1, lambda_q2, lambda_k2, diff_subln_w,
           w_out, norm_mix_post, norm_xattn_pre, norm_mem, w_xq, w_xkv, w_xo, norm_xattn_post,
           norm_ffn_pre, w_ffn_gate, w_ffn_up, w_ffn_down, norm_ffn_post):
    stacked = dict(
        norm_mix_pre=norm_mix_pre, w_in=w_in, w_gate_up_f=w_gate_up_f, b_gate_f=b_gate_f,
        w_gate_up_b=w_gate_up_b, b_gate_b=b_gate_b, gla_norm_w=gla_norm_w, lambda_q1=lambda_q1,
        lambda_k1=lambda_k1, lambda_q2=lambda_q2, lambda_k2=lambda_k2, diff_subln_w=diff_subln_w,
        w_out=w_out, norm_mix_post=norm_mix_post, norm_xattn_pre=norm_xattn_pre, norm_mem=norm_mem,
        w_xq=w_xq, w_xkv=w_xkv, w_xo=w_xo, norm_xattn_post=norm_xattn_post,
        norm_ffn_pre=norm_ffn_pre, w_ffn_gate=w_ffn_gate, w_ffn_up=w_ffn_up, w_ffn_down=w_ffn_down,
        norm_ffn_post=norm_ffn_post)
    y_prompt, y_sample = x_prompt, x_sample
    for layer in range(w_in.shape[0]):
        lam_init = 0.8 - 0.6 * math.exp(-0.3 * layer)
        w = _prepare_weights({name: t[layer] for name, t in stacked.items()})
        y_prompt = _encoder_layer(y_prompt, mem_prompt, w, lam_init)
        y_sample = _encoder_layer(y_sample, mem_sample, w, lam_init)
    return (y_prompt, y_sample)
```

```python
import functools
import math

import numpy as np
import jax
import jax.numpy as jnp
from jax import lax
from jax.experimental import pallas as pl
from jax.experimental.pallas import tpu as pltpu

F32 = jnp.float32
BF16 = jnp.bfloat16

D_MODEL = 1024
GLA_HEADS = 4
GLA_DK = 64
GLA_DV = 128
GLA_QK = GLA_HEADS * GLA_DK
GLA_V = GLA_HEADS * GLA_DV
GLA_GATE_RANK = 16
GLA_GATE_NORMALIZER = 16.0
GLA_CHUNK = 64
DIFF_HEADS = 4
DIFF_DH = 64
DIFF_QK = DIFF_HEADS * 2 * DIFF_DH
DIFF_V = DIFF_HEADS * 2 * DIFF_DH
ROT_DIM = DIFF_DH // 4
ROPE_THETA = 500000.0
MEM_LEN = 256
XATTN_HEADS = 4
XATTN_DH = D_MODEL // XATTN_HEADS
D_FF = 2816
RMS_EPS = 1e-6

LANES = 128
TOKEN_TILE = 256
GLA_BLOCK = 512
INPROJ_TILE = 1024
PREFIX_ROWS = 256
ATTN_Q_TILE = 512
ATTN_K_TILE = 2048
MIX_ROW_SPLIT = 4
ATTN_KEY_BLOCK = 256
LOG2E = math.log2(math.e)
VMEM_LIMIT = 56 * 1024 * 1024

NT_DIMS = (((1,), (1,)), ((), ()))
TN_DIMS = (((0,), (0,)), ((), ()))


def _rms(x, w):
    return x * lax.rsqrt(jnp.mean(x * x, axis=-1, keepdims=True) + RMS_EPS) * w


def _dot(a, b):
    return jnp.dot(a, b, preferred_element_type=F32)


def _dot_nt(a, b):
    return lax.dot_general(a, b, NT_DIMS, preferred_element_type=F32)


def _dot_tn(a, b):
    return lax.dot_general(a, b, TN_DIMS, preferred_element_type=F32)


def _const_spec(shape):
    return pl.BlockSpec(shape, lambda *_: (0,) * len(shape), pipeline_mode=pl.Buffered(1))


def _params(semantics):
    return pltpu.CompilerParams(dimension_semantics=semantics, vmem_limit_bytes=VMEM_LIMIT)


def _mem_kv_kernel(m_ref, nw_ref, w_ref, k_ref, v_ref):
    m = _rms(m_ref[...], nw_ref[...]).astype(BF16)
    kv = _dot(m, w_ref[...])
    k_ref[...] = kv[:, :D_MODEL].astype(BF16)
    v_ref[...] = kv[:, D_MODEL:].astype(BF16)


def _mem_kv(mem2d, norm_w, w_xkv):
    rows = mem2d.shape[0]
    out = jax.ShapeDtypeStruct((rows, D_MODEL), BF16)
    return pl.pallas_call(
        _mem_kv_kernel,
        grid=(rows // MEM_LEN,),
        in_specs=[pl.BlockSpec((MEM_LEN, D_MODEL), lambda i: (i, 0)),
                  _const_spec((1, D_MODEL)),
                  _const_spec((D_MODEL, 2 * D_MODEL))],
        out_specs=[pl.BlockSpec((MEM_LEN, D_MODEL), lambda i: (i, 0))] * 2,
        out_shape=[out, out],
        compiler_params=_params(("parallel",)),
        name="mem_kv",
    )(mem2d, norm_w, w_xkv)


def _inproj_kernel(x_ref, nw_ref, wa_ref, wg_ref, wup_ref, bg_ref, tri_ref, cos_ref, sin_ref,
                   f_ref, b_ref, v_ref, og_ref, decf_ref, decb_ref, dq_ref, dk_ref, dv_ref):
    for j in range(x_ref.shape[0] // GLA_BLOCK):
        rows = pl.ds(j * GLA_BLOCK, GLA_BLOCK)
        blk = pl.ds(j, 1)
        _inproj_rows(x_ref.at[rows], nw_ref, wa_ref, wg_ref, wup_ref, bg_ref, tri_ref,
                     cos_ref.at[rows], sin_ref.at[rows], f_ref.at[rows], b_ref.at[rows],
                     v_ref.at[rows], og_ref.at[rows], decf_ref.at[blk], decb_ref.at[blk],
                     dq_ref.at[rows], dk_ref.at[rows], dv_ref.at[rows])


def _inproj_rows(x_ref, nw_ref, wa_ref, wg_ref, wup_ref, bg_ref, tri_ref, cos_ref, sin_ref,
                 f_ref, b_ref, v_ref, og_ref, decf_ref, decb_ref, dq_ref, dk_ref, dv_ref):
    tm = x_ref.shape[0]
    h = _rms(x_ref[...], nw_ref[...]).astype(BF16)

    def proj(off, width):
        return _dot(h, wa_ref[:, off:off + width])

    gdown = _dot(h, wg_ref[...]).astype(BF16)
    pre = _dot(gdown, wup_ref[...]) + bg_ref[...]

    off = 2 * GLA_QK
    v_ref[...] = proj(off, GLA_V).astype(BF16)
    off += GLA_V
    og_ref[...] = proj(off, GLA_V).astype(BF16)
    off += GLA_V

    cos = cos_ref[...]
    sin = sin_ref[...]
    lane = lax.broadcasted_iota(jnp.int32, (tm, LANES), 1)
    second_half = (lane & (DIFF_DH - 1)) >= (ROT_DIM // 2)

    def rope(t):
        slabs = []
        for s in range(t.shape[1] // LANES):
            ts = t[:, s * LANES:(s + 1) * LANES]
            partner = jnp.where(second_half, pltpu.roll(ts, ROT_DIM // 2, 1),
                                pltpu.roll(ts, LANES - ROT_DIM // 2, 1))
            slabs.append(ts * cos + partner * sin)
        return jnp.concatenate(slabs, axis=1)

    dq = proj(off, DIFF_QK)
    dq_ref[...] = (rope(dq) * (DIFF_DH ** -0.5 * LOG2E)).astype(BF16)
    off += DIFF_QK
    dk = proj(off, DIFF_QK)
    dk_ref[...] = rope(dk).astype(BF16)
    off += DIFF_QK
    dv_ref[...] = proj(off, DIFF_V).astype(BF16)

    g = (jnp.minimum(pre, 0.0) - jnp.log1p(jnp.exp(-jnp.abs(pre)))) * (1.0 / GLA_GATE_NORMALIZER)
    g_hi = g.astype(BF16)
    g_lo = (g - g_hi.astype(F32)).astype(BF16)
    g_split = jnp.concatenate([g_hi, g_lo], axis=1)
    tri = tri_ref[...]
    half = tri.shape[0]
    pre_sum = jnp.concatenate([_dot(tri, g_split[r:r + half]) for r in range(0, tm, half)], axis=0)
    prefix = pre_sum[:, :2 * GLA_QK] + pre_sum[:, 2 * GLA_QK:]
    totals = [prefix[c + GLA_CHUNK - 1:c + GLA_CHUNK] for c in range(0, tm, GLA_CHUNK)]
    total = jnp.concatenate([jnp.broadcast_to(t, (GLA_CHUNK, 2 * GLA_QK)) for t in totals], axis=0)
    dec = jnp.exp(jnp.concatenate(totals, axis=0))
    pf, pb = prefix[:, :GLA_QK], prefix[:, GLA_QK:]
    tf, tb = total[:, :GLA_QK], total[:, GLA_QK:]
    gb = g[:, GLA_QK:]

    qk = proj(0, 2 * GLA_QK)
    q = qk[:, :GLA_QK] * (GLA_DK ** -0.5)
    k = qk[:, GLA_QK:]
    for cum, rest, o_ref in ((pf, tf - pf, f_ref), (tb - pb + gb, pb - gb, b_ref)):
        o_ref[:, 0:GLA_QK] = (q * jnp.exp(cum)).astype(BF16)
        o_ref[:, GLA_QK:2 * GLA_QK] = (k * jnp.exp(-cum)).astype(BF16)
        o_ref[:, 2 * GLA_QK:3 * GLA_QK] = (k * jnp.exp(rest)).astype(BF16)
    decf_ref[0] = dec[:, :GLA_QK].T
    decb_ref[0] = dec[:, GLA_QK:].T


def _chunk_prefix_matrix(rows):
    r = np.arange(rows)[:, None]
    c = np.arange(rows)[None, :]
    return jnp.asarray(((r // GLA_CHUNK) == (c // GLA_CHUNK)) & (c <= r), BF16)


def _rope_tables(seq):
    inv = ROPE_THETA ** (-jnp.arange(0, ROT_DIM, 2, dtype=F32) / ROT_DIM)
    ang = jnp.arange(seq).astype(F32)[:, None] * inv[None, :]
    cos, sin = jnp.cos(ang), jnp.sin(ang)
    rest = DIFF_DH - ROT_DIM
    cos64 = jnp.concatenate([cos, cos, jnp.ones((seq, rest), F32)], axis=1)
    sin64 = jnp.concatenate([-sin, sin, jnp.zeros((seq, rest), F32)], axis=1)
    reps = LANES // DIFF_DH
    return jnp.tile(cos64, (1, reps)), jnp.tile(sin64, (1, reps))


def _in_proj(x2d, seq, w):
    tokens = x2d.shape[0]
    tm = INPROJ_TILE
    nchunk = GLA_BLOCK // GLA_CHUNK
    tri = _chunk_prefix_matrix(PREFIX_ROWS)
    cos, sin = _rope_tables(seq)
    pos_blocks = seq // tm
    row = lambda i: (i, 0)
    pos = lambda i: (i % pos_blocks, 0)

    def tok(width, dtype=BF16):
        return jax.ShapeDtypeStruct((tokens, width), dtype)

    dec = jax.ShapeDtypeStruct((tokens // GLA_BLOCK, GLA_QK, nchunk), F32)
    dec_spec = pl.BlockSpec((tm // GLA_BLOCK, GLA_QK, nchunk), lambda i: (i, 0, 0))
    return pl.pallas_call(
        _inproj_kernel,
        grid=(tokens // tm,),
        in_specs=[pl.BlockSpec((tm, D_MODEL), row),
                  _const_spec((1, D_MODEL)),
                  _const_spec(w["w_a"].shape),
                  _const_spec(w["w_g"].shape),
                  _const_spec(w["w_up"].shape),
                  _const_spec(w["b_g"].shape),
                  _const_spec(tri.shape),
                  pl.BlockSpec((tm, LANES), pos),
                  pl.BlockSpec((tm, LANES), pos)],
        out_specs=[pl.BlockSpec((tm, 3 * GLA_QK), row),
                   pl.BlockSpec((tm, 3 * GLA_QK), row),
                   pl.BlockSpec((tm, GLA_V), row),
                   pl.BlockSpec((tm, GLA_V), row),
                   dec_spec,
                   dec_spec,
                   pl.BlockSpec((tm, DIFF_QK), row),
                   pl.BlockSpec((tm, DIFF_QK), row),
                   pl.BlockSpec((tm, DIFF_V), row)],
        out_shape=[tok(3 * GLA_QK), tok(3 * GLA_QK), tok(GLA_V), tok(GLA_V), dec, dec,
                   tok(DIFF_QK), tok(DIFF_QK), tok(DIFF_V)],
        compiler_params=_params(("parallel",)),
        name="in_proj",
    )(x2d, w["norm_mix_pre"], w["w_a"], w["w_g"], w["w_up"], w["b_g"], tri, cos, sin)


def _gla_kernel(f_ref, b_ref, vf_ref, vb_ref, decf_ref, decb_ref, of_ref, ob_ref, sf_ref, sb_ref):
    nchunk = f_ref.shape[0] // GLA_CHUNK

    @pl.when(pl.program_id(1) == 0)
    def _():
        sf_ref[...] = jnp.zeros_like(sf_ref)
        sb_ref[...] = jnp.zeros_like(sb_ref)

    qk_head = lax.broadcasted_iota(jnp.int32, (GLA_CHUNK, GLA_QK), 1) // GLA_DK
    v_head = lax.broadcasted_iota(jnp.int32, (GLA_CHUNK, GLA_V), 1) // GLA_DV
    s_head = lax.broadcasted_iota(jnp.int32, (GLA_QK, GLA_DV), 0) // GLA_DK
    a_row = lax.broadcasted_iota(jnp.int32, (GLA_CHUNK, GLA_QK), 0)
    a_col = lax.broadcasted_iota(jnp.int32, (GLA_CHUNK, GLA_QK), 1) % GLA_CHUNK

    def chunk_step(x_ref, v_ref, dec_ref, o_ref, s_ref, c, keep):
        rows = slice(c * GLA_CHUNK, (c + 1) * GLA_CHUNK)
        qe = x_ref[rows, 0:GLA_QK]
        ke = x_ref[rows, GLA_QK:2 * GLA_QK]
        kd = x_ref[rows, 2 * GLA_QK:3 * GLA_QK]
        v = v_ref[rows, :]
        dec = dec_ref[0, :, c:c + 1]
        ke_heads = jnp.concatenate([jnp.where(qk_head == h, ke, jnp.zeros_like(ke))
                                    for h in range(GLA_HEADS)], axis=0)
        a = jnp.where(keep, _dot_nt(qe, ke_heads), 0.0).astype(BF16)
        v_heads = jnp.concatenate([jnp.where(v_head == h, v, jnp.zeros_like(v))
                                   for h in range(GLA_HEADS)], axis=0)
        state = s_ref[...]
        state16 = state.astype(BF16)
        s_heads = jnp.concatenate([jnp.where(s_head == h, state16, jnp.zeros_like(state16))
                                   for h in range(GLA_HEADS)], axis=1)
        o_ref[rows, :] = (_dot(a, v_heads) + _dot(qe, s_heads)).astype(BF16)
        upd = jnp.concatenate([_dot_tn(kd[:, h * GLA_DK:(h + 1) * GLA_DK],
                                       v[:, h * GLA_DV:(h + 1) * GLA_DV])
                               for h in range(GLA_HEADS)], axis=0)
        s_ref[...] = dec * state + upd

    for c in range(nchunk):
        chunk_step(f_ref, vf_ref, decf_ref, of_ref, sf_ref, c, a_col <= a_row)
        chunk_step(b_ref, vb_ref, decb_ref, ob_ref, sb_ref, nchunk - 1 - c, a_col >= a_row)


def _gla(f, b, v, decf, decb, batch, seq):
    tokens = f.shape[0]
    lb = GLA_BLOCK
    nblk = seq // lb
    nchunk = lb // GLA_CHUNK
    fwd = lambda bi, i: (bi * nblk + i, 0)
    bwd = lambda bi, i: (bi * nblk + nblk - 1 - i, 0)
    fwd3 = lambda bi, i: (bi * nblk + i, 0, 0)
    bwd3 = lambda bi, i: (bi * nblk + nblk - 1 - i, 0, 0)
    out = jax.ShapeDtypeStruct((tokens, GLA_V), BF16)
    return pl.pallas_call(
        _gla_kernel,
        grid=(batch, nblk),
        in_specs=[pl.BlockSpec((lb, 3 * GLA_QK), fwd),
                  pl.BlockSpec((lb, 3 * GLA_QK), bwd),
                  pl.BlockSpec((lb, GLA_V), fwd),
                  pl.BlockSpec((lb, GLA_V), bwd),
                  pl.BlockSpec((1, GLA_QK, nchunk), fwd3),
                  pl.BlockSpec((1, GLA_QK, nchunk), bwd3)],
        out_specs=[pl.BlockSpec((lb, GLA_V), fwd),
                   pl.BlockSpec((lb, GLA_V), bwd)],
        out_shape=[out, out],
        scratch_shapes=[pltpu.VMEM((GLA_QK, GLA_DV), F32),
                        pltpu.VMEM((GLA_QK, GLA_DV), F32)],
        compiler_params=_params(("parallel", "arbitrary")),
        name="gla",
    )(f, b, v, v, decf, decb)


def _diff_kernel(q_ref, k_ref, v_ref, lq1_ref, lk1_ref, lq2_ref, lk2_ref, w_ref, o_ref,
                 qm_ref, m_ref, l_ref, acc_ref, s_ref, *, lam_init):
    ki = pl.program_id(2)
    tq = q_ref.shape[0]
    tk = k_ref.shape[0]
    width = 2 * DIFF_DH

    @pl.when(ki == 0)
    def _():
        m_ref[...] = jnp.full_like(m_ref, -jnp.inf)
        l_ref[...] = jnp.zeros_like(l_ref)
        acc_ref[...] = jnp.zeros_like(acc_ref)
        first = lax.broadcasted_iota(jnp.int32, (tq, width), 1) < DIFF_DH
        for h in range(DIFF_HEADS):
            qh = q_ref[:, h * width:(h + 1) * width]
            zero = jnp.zeros_like(qh)
            qm_ref[2 * h] = jnp.where(first, qh, zero)
            qm_ref[2 * h + 1] = jnp.where(first, zero, qh)

    ones = jnp.ones((ATTN_KEY_BLOCK, width), BF16)
    nstream = 2 * DIFF_HEADS

    def scores(s_idx):
        h = s_idx // 2
        s = _dot_nt(qm_ref[s_idx], k_ref[:, h * width:(h + 1) * width])
        s_ref[s_idx % 2] = s
        m_prev = m_ref[s_idx]
        m_next = jnp.maximum(m_prev, jnp.max(s, axis=1, keepdims=True))
        m_ref[s_idx] = m_next
        return m_prev, m_next

    maxes = scores(0)
    for s_idx in range(nstream):
        m_prev, m_next = maxes
        if s_idx + 1 < nstream:
            maxes = scores(s_idx + 1)
        h = s_idx // 2
        vh = v_ref[:, h * width:(h + 1) * width]
        m_wide = jnp.tile(m_next, (1, ATTN_KEY_BLOCK // LANES))
        pv = jnp.zeros((tq, 2 * width), F32)
        for k0 in range(0, tk, ATTN_KEY_BLOCK):
            keys = slice(k0, k0 + ATTN_KEY_BLOCK)
            pb = jnp.exp2(s_ref[s_idx % 2, :, keys] - m_wide).astype(BF16)
            pv = pv + _dot(pb, jnp.concatenate([vh[keys], ones], axis=1))
        alpha = jnp.exp2(m_prev - m_next)
        acc_ref[s_idx] = alpha * acc_ref[s_idx] + pv[:, :width]
        l_ref[s_idx] = alpha * l_ref[s_idx] + pv[:, width:]

    @pl.when(ki == pl.num_programs(2) - 1)
    def _():
        lam = (jnp.exp(jnp.sum(lq1_ref[...] * lk1_ref[...], axis=1, keepdims=True))
               - jnp.exp(jnp.sum(lq2_ref[...] * lk2_ref[...], axis=1, keepdims=True)) + lam_init)
        for h in range(DIFF_HEADS):
            o1 = acc_ref[2 * h] / l_ref[2 * h]
            o2 = acc_ref[2 * h + 1] / l_ref[2 * h + 1]
            o = _rms(o1 - lam * o2, w_ref[...]) * (1.0 - lam_init)
            o_ref[:, h * width:(h + 1) * width] = o.astype(BF16)


def _diff_attn(dq, dk, dv, w, batch, seq, lam_init):
    tokens = dq.shape[0]
    tq = min(ATTN_Q_TILE, seq)
    tk = min(ATTN_K_TILE, seq)
    nq, nk = seq // tq, seq // tk
    qmap = lambda b, qi, ki: (b * nq + qi, 0)
    kmap = lambda b, qi, ki: (b * nk + ki, 0)
    nstream = 2 * DIFF_HEADS
    vec = _const_spec((1, DIFF_DH))
    return pl.pallas_call(
        functools.partial(_diff_kernel, lam_init=lam_init),
        grid=(batch, nq, nk),
        in_specs=[pl.BlockSpec((tq, DIFF_QK), qmap),
                  pl.BlockSpec((tk, DIFF_QK), kmap),
                  pl.BlockSpec((tk, DIFF_V), kmap),
                  vec, vec, vec, vec,
                  _const_spec((1, 2 * DIFF_DH))],
        out_specs=pl.BlockSpec((tq, DIFF_V), qmap),
        out_shape=jax.ShapeDtypeStruct((tokens, DIFF_V), BF16),
        scratch_shapes=[pltpu.VMEM((nstream, tq, 2 * DIFF_DH), BF16),
                        pltpu.VMEM((nstream, tq, LANES), F32),
                        pltpu.VMEM((nstream, tq, LANES), F32),
                        pltpu.VMEM((nstream, tq, 2 * DIFF_DH), F32),
                        pltpu.VMEM((2, tq, tk), F32)],
        compiler_params=_params(("parallel", "parallel", "arbitrary")),
        name="diff_attn",
    )(dq, dk, dv, w["lambda_q1"], w["lambda_k1"], w["lambda_q2"], w["lambda_k2"], w["diff_subln_w"])


def _mix_xattn_kernel(x_ref, of_ref, ob_ref, og_ref, od_ref, k_ref, v_ref,
                      gnw_ref, wout_ref, npost_ref, nxpre_ref, wxq_ref, wxo_ref, nxpost_ref, y_ref):
    tm = x_ref.shape[0]
    parts = [slice(r, r + tm // MIX_ROW_SPLIT) for r in range(0, tm, tm // MIX_ROW_SPLIT)]

    def gla_out(rows):
        o = of_ref[rows, :].astype(F32) + ob_ref[rows, :].astype(F32)
        og = og_ref[rows, :].astype(F32)
        gate = og * (1.0 / (1.0 + jnp.exp(-og)))
        heads = []
        for h in range(GLA_HEADS):
            cols = slice(h * GLA_DV, (h + 1) * GLA_DV)
            heads.append((_rms(o[:, cols], gnw_ref[...]) * gate[:, cols]).astype(BF16))
        return jnp.concatenate(heads + [od_ref[rows, :]], axis=1)

    def attend(q):
        outs = []
        for h in range(XATTN_HEADS):
            cols = slice(h * XATTN_DH, (h + 1) * XATTN_DH)
            s = _dot_nt(q[:, cols], k_ref[:, cols])
            p = jnp.exp(s - jnp.max(s, axis=1, keepdims=True))
            denom = jnp.sum(p, axis=1, keepdims=True)
            outs.append((_dot(p.astype(BF16), v_ref[:, cols]) / denom).astype(BF16))
        return jnp.concatenate(outs, axis=1)

    mix_in = [gla_out(rows) for rows in parts]
    mix = [_dot(t, wout_ref[...]) for t in mix_in]
    x1 = [x_ref[rows, :] + _rms(t, npost_ref[...]) for rows, t in zip(parts, mix)]
    h2 = [_rms(t, nxpre_ref[...]).astype(BF16) for t in x1]
    q = [(_dot(t, wxq_ref[...]) * (XATTN_DH ** -0.5)).astype(BF16) for t in h2]
    att = [attend(t) for t in q]
    xo = [_dot(t, wxo_ref[...]) for t in att]
    for rows, res, t in zip(parts, x1, xo):
        y_ref[rows, :] = res + _rms(t, nxpost_ref[...])


def _mix_xattn(x2d, o_f, o_b, og, o_diff, mem_k, mem_v, seq, w):
    tokens = x2d.shape[0]
    tm = MIX_ROW_SPLIT * TOKEN_TILE
    per_seq = seq // tm
    row = lambda i: (i, 0)
    memmap = lambda i: (i // per_seq, 0)
    half = pl.BlockSpec((tm, GLA_V), row)
    sq = _const_spec((D_MODEL, D_MODEL))
    nv = _const_spec((1, D_MODEL))
    return pl.pallas_call(
        _mix_xattn_kernel,
        grid=(tokens // tm,),
        in_specs=[pl.BlockSpec((tm, D_MODEL), row), half, half, half, half,
                  pl.BlockSpec((MEM_LEN, D_MODEL), memmap),
                  pl.BlockSpec((MEM_LEN, D_MODEL), memmap),
                  _const_spec((1, GLA_DV)), sq, nv, nv, sq, sq, nv],
        out_specs=pl.BlockSpec((tm, D_MODEL), row),
        out_shape=jax.ShapeDtypeStruct((tokens, D_MODEL), F32),
        compiler_params=_params(("parallel",)),
        name="mix_xattn",
    )(x2d, o_f, o_b, og, o_diff, mem_k, mem_v, w["gla_norm_w"], w["w_out"], w["norm_mix_post"],
      w["norm_xattn_pre"], w["w_xq"], w["w_xo"], w["norm_xattn_post"])


def _ffn_kernel(x_ref, npre_ref, wg_ref, wu_ref, wd_ref, npost_ref, y_ref):
    tm = x_ref.shape[0]
    parts = [slice(r, r + tm // MIX_ROW_SPLIT) for r in range(0, tm, tm // MIX_ROW_SPLIT)]
    h = [_rms(x_ref[rows, :], npre_ref[...]).astype(BF16) for rows in parts]
    g = [_dot(t, wg_ref[...]) for t in h]
    u = [_dot(t, wu_ref[...]) for t in h]
    a = [(tg * (1.0 / (1.0 + jnp.exp(-tg))) * tu).astype(BF16) for tg, tu in zip(g, u)]
    d = [_dot(t, wd_ref[...]) for t in a]
    for rows, t in zip(parts, d):
        y_ref[rows, :] = x_ref[rows, :] + _rms(t, npost_ref[...])


def _ffn(x2d, w):
    tokens = x2d.shape[0]
    tm = MIX_ROW_SPLIT * TOKEN_TILE
    row = lambda i: (i, 0)
    nv = _const_spec((1, D_MODEL))
    return pl.pallas_call(
        _ffn_kernel,
        grid=(tokens // tm,),
        in_specs=[pl.BlockSpec((tm, D_MODEL), row), nv,
                  _const_spec((D_MODEL, D_FF)), _const_spec((D_MODEL, D_FF)),
                  _const_spec((D_FF, D_MODEL)), nv],
        out_specs=pl.BlockSpec((tm, D_MODEL), row),
        out_shape=jax.ShapeDtypeStruct((tokens, D_MODEL), F32),
        compiler_params=_params(("parallel",)),
        name="ffn",
    )(x2d, w["norm_ffn_pre"], w["w_ffn_gate"], w["w_ffn_up"], w["w_ffn_down"], w["norm_ffn_post"])


def _prepare_weights(p):
    w_in = p["w_in"]
    gates_at = 2 * GLA_QK + GLA_V
    gates_end = gates_at + 2 * GLA_GATE_RANK
    w = {}
    w["w_a"] = jnp.concatenate([w_in[:, :gates_at], w_in[:, gates_end:]], axis=1).astype(BF16)
    w["w_g"] = jnp.pad(w_in[:, gates_at:gates_end], ((0, 0), (0, LANES - 2 * GLA_GATE_RANK))).astype(BF16)
    w_up = jnp.zeros((LANES, 2 * GLA_QK), F32)
    w_up = w_up.at[:GLA_GATE_RANK, :GLA_QK].set(p["w_gate_up_f"])
    w_up = w_up.at[GLA_GATE_RANK:2 * GLA_GATE_RANK, GLA_QK:].set(p["w_gate_up_b"])
    w["w_up"] = w_up.astype(BF16)
    w["b_g"] = jnp.concatenate([p["b_gate_f"], p["b_gate_b"]])[None, :]
    for name in ("w_out", "w_xq", "w_xkv", "w_xo", "w_ffn_gate", "w_ffn_up", "w_ffn_down"):
        w[name] = p[name].astype(BF16)
    for name in ("norm_mix_pre", "gla_norm_w", "lambda_q1", "lambda_k1", "lambda_q2", "lambda_k2",
                 "diff_subln_w", "norm_mix_post", "norm_xattn_pre", "norm_mem", "norm_xattn_post",
                 "norm_ffn_pre", "norm_ffn_post"):
        w[name] = p[name][None, :]
    return w


def _encoder_layer(x, mem, w, lam_init):
    batch, seq, _ = x.shape
    assert seq % (MIX_ROW_SPLIT * TOKEN_TILE) == 0 and seq % INPROJ_TILE == 0 and mem.shape[1] == MEM_LEN
    x2d = x.reshape(batch * seq, D_MODEL)
    mem_k, mem_v = _mem_kv(mem.reshape(batch * MEM_LEN, D_MODEL), w["norm_mem"], w["w_xkv"])
    f, b, v, og, decf, decb, dq, dk, dv = _in_proj(x2d, seq, w)
    o_f, o_b = _gla(f, b, v, decf, decb, batch, seq)
    o_diff = _diff_attn(dq, dk, dv, w, batch, seq, lam_init)
    x2 = _mix_xattn(x2d, o_f, o_b, og, o_diff, mem_k, mem_v, seq, w)
    return _ffn(x2, w).reshape(batch, seq, D_MODEL)


def kernel(x_prompt, x_sample, mem_prompt, mem_sample, norm_mix_pre, w_in, w_gate_up_f, b_gate_f,
           w_gate_up_b, b_gate_b, gla_norm_w, lambda_q1, lambda_k1, lambda_q2, lambda_k2, diff_subln_w,
           w_out, norm_mix_post, norm_xattn_pre, norm_mem, w_xq, w_xkv, w_xo, norm_xattn_post,
           norm_ffn_pre, w_ffn_gate, w_ffn_up, w_ffn_down, norm_ffn_post):
    stacked = dict(
        norm_mix_pre=norm_mix_pre, w_in=w_in, w_gate_up_f=w_gate_up_f, b_gate_f=b_gate_f,
        w_gate_up_b=w_gate_up_b, b_gate_b=b_gate_b, gla_norm_w=gla_norm_w, lambda_q1=lambda_q1,
        lambda_k1=lambda_k1, lambda_q2=lambda_q2, lambda_k2=lambda_k2, diff_subln_w=diff_subln_w,
        w_out=w_out, norm_mix_post=norm_mix_post, norm_xattn_pre=norm_xattn_pre, norm_mem=norm_mem,
        w_xq=w_xq, w_xkv=w_xkv, w_xo=w_xo, norm_xattn_post=norm_xattn_post,
        norm_ffn_pre=norm_ffn_pre, w_ffn_gate=w_ffn_gate, w_ffn_up=w_ffn_up, w_ffn_down=w_ffn_down,
        norm_ffn_post=norm_ffn_post)
    y_prompt, y_sample = x_prompt, x_sample
    for layer in range(w_in.shape[0]):
        lam_init = 0.8 - 0.6 * math.exp(-0.3 * layer)
        w = _prepare_weights({name: t[layer] for name, t in stacked.items()})
        y_prompt = _encoder_layer(y_prompt, mem_prompt, w, lam_init)
        y_sample = _encoder_layer(y_sample, mem_sample, w, lam_init)
    return (y_prompt, y_sample)
```

```python
import functools
import math

import numpy as np
import jax
import jax.numpy as jnp
from jax import lax
from jax.experimental import pallas as pl
from jax.experimental.pallas import tpu as pltpu

F32 = jnp.float32
BF16 = jnp.bfloat16

D_MODEL = 1024
GLA_HEADS = 4
GLA_DK = 64
GLA_DV = 128
GLA_QK = GLA_HEADS * GLA_DK
GLA_V = GLA_HEADS * GLA_DV
GLA_GATE_RANK = 16
GLA_GATE_NORMALIZER = 16.0
GLA_CHUNK = 64
DIFF_HEADS = 4
DIFF_DH = 64
DIFF_QK = DIFF_HEADS * 2 * DIFF_DH
DIFF_V = DIFF_HEADS * 2 * DIFF_DH
ROT_DIM = DIFF_DH // 4
ROPE_THETA = 500000.0
MEM_LEN = 256
XATTN_HEADS = 4
XATTN_DH = D_MODEL // XATTN_HEADS
D_FF = 2816
RMS_EPS = 1e-6

LANES = 128
TOKEN_TILE = 256
GLA_BLOCK = 512
INPROJ_TILE = 1024
INPROJ_ROWS = 256
PREFIX_ROWS = 256
DEC_LANES = 8
ATTN_Q_TILE = 512
ATTN_K_TILE = 2048
MIX_ROW_SPLIT = 4
ATTN_KEY_BLOCK = 256
LOG2E = math.log2(math.e)
VMEM_LIMIT = 56 * 1024 * 1024

NT_DIMS = (((1,), (1,)), ((), ()))
TN_DIMS = (((0,), (0,)), ((), ()))


def _rms(x, w):
    return x * lax.rsqrt(jnp.mean(x * x, axis=-1, keepdims=True) + RMS_EPS) * w


def _dot(a, b):
    return jnp.dot(a, b, preferred_element_type=F32)


def _dot_nt(a, b):
    return lax.dot_general(a, b, NT_DIMS, preferred_element_type=F32)


def _dot_tn(a, b):
    return lax.dot_general(a, b, TN_DIMS, preferred_element_type=F32)


def _const_spec(shape):
    return pl.BlockSpec(shape, lambda *_: (0,) * len(shape), pipeline_mode=pl.Buffered(1))


def _params(semantics):
    return pltpu.CompilerParams(dimension_semantics=semantics, vmem_limit_bytes=VMEM_LIMIT)


def _mem_kv_kernel(m_ref, nw_ref, w_ref, k_ref, v_ref):
    m = _rms(m_ref[...], nw_ref[...]).astype(BF16)
    kv = _dot(m, w_ref[...])
    k_ref[...] = kv[:, :D_MODEL].astype(BF16)
    v_ref[...] = kv[:, D_MODEL:].astype(BF16)


def _mem_kv(mem2d, norm_w, w_xkv):
    rows = mem2d.shape[0]
    out = jax.ShapeDtypeStruct((rows, D_MODEL), BF16)
    return pl.pallas_call(
        _mem_kv_kernel,
        grid=(rows // MEM_LEN,),
        in_specs=[pl.BlockSpec((MEM_LEN, D_MODEL), lambda i: (i, 0)),
                  _const_spec((1, D_MODEL)),
                  _const_spec((D_MODEL, 2 * D_MODEL))],
        out_specs=[pl.BlockSpec((MEM_LEN, D_MODEL), lambda i: (i, 0))] * 2,
        out_shape=[out, out],
        compiler_params=_params(("parallel",)),
        name="mem_kv",
    )(mem2d, norm_w, w_xkv)


def _inproj_kernel(x_ref, nw_ref, wa_ref, wg_ref, wup_ref, bg_ref, tri_ref, cos_ref, sin_ref,
                   f_ref, b_ref, v_ref, og_ref, decf_ref, decb_ref, dq_ref, dk_ref, dv_ref):
    for j in range(x_ref.shape[0] // INPROJ_ROWS):
        rows = pl.ds(j * INPROJ_ROWS, INPROJ_ROWS)
        blk = pl.ds(j, 1)
        _inproj_rows(x_ref.at[rows], nw_ref, wa_ref, wg_ref, wup_ref, bg_ref, tri_ref,
                     cos_ref.at[rows], sin_ref.at[rows], f_ref.at[rows], b_ref.at[rows],
                     v_ref.at[rows], og_ref.at[rows], decf_ref.at[blk], decb_ref.at[blk],
                     dq_ref.at[rows], dk_ref.at[rows], dv_ref.at[rows])


def _inproj_rows(x_ref, nw_ref, wa_ref, wg_ref, wup_ref, bg_ref, tri_ref, cos_ref, sin_ref,
                 f_ref, b_ref, v_ref, og_ref, decf_ref, decb_ref, dq_ref, dk_ref, dv_ref):
    tm = x_ref.shape[0]
    h = _rms(x_ref[...], nw_ref[...]).astype(BF16)

    def proj(off, width):
        return _dot(h, wa_ref[:, off:off + width])

    gdown = _dot(h, wg_ref[...]).astype(BF16)
    pre = _dot(gdown, wup_ref[...]) + bg_ref[...]

    off = 2 * GLA_QK
    v_ref[...] = proj(off, GLA_V).astype(BF16)
    off += GLA_V
    og_ref[...] = proj(off, GLA_V).astype(BF16)
    off += GLA_V

    cos = cos_ref[...]
    sin = sin_ref[...]
    lane = lax.broadcasted_iota(jnp.int32, (tm, LANES), 1)
    second_half = (lane & (DIFF_DH - 1)) >= (ROT_DIM // 2)

    def rope(t):
        slabs = []
        for s in range(t.shape[1] // LANES):
            ts = t[:, s * LANES:(s + 1) * LANES]
            partner = jnp.where(second_half, pltpu.roll(ts, ROT_DIM // 2, 1),
                                pltpu.roll(ts, LANES - ROT_DIM // 2, 1))
            slabs.append(ts * cos + partner * sin)
        return jnp.concatenate(slabs, axis=1)

    dq = proj(off, DIFF_QK)
    dq_ref[...] = (rope(dq) * (DIFF_DH ** -0.5 * LOG2E)).astype(BF16)
    off += DIFF_QK
    dk = proj(off, DIFF_QK)
    dk_ref[...] = rope(dk).astype(BF16)
    off += DIFF_QK
    dv_ref[...] = proj(off, DIFF_V).astype(BF16)

    g = (jnp.minimum(pre, 0.0) - jnp.log1p(jnp.exp(-jnp.abs(pre)))) * (1.0 / GLA_GATE_NORMALIZER)
    g_hi = g.astype(BF16)
    g_lo = (g - g_hi.astype(F32)).astype(BF16)
    g_split = jnp.concatenate([g_hi, g_lo], axis=1)
    tri = tri_ref[...]
    half = tri.shape[0]
    pre_sum = jnp.concatenate([_dot(tri, g_split[r:r + half]) for r in range(0, tm, half)], axis=0)
    prefix = pre_sum[:, :2 * GLA_QK] + pre_sum[:, 2 * GLA_QK:]
    totals = [prefix[c + GLA_CHUNK - 1:c + GLA_CHUNK] for c in range(0, tm, GLA_CHUNK)]
    total = jnp.concatenate([jnp.broadcast_to(t, (GLA_CHUNK, 2 * GLA_QK)) for t in totals], axis=0)
    dec = jnp.exp(jnp.concatenate(totals, axis=0))
    pf, pb = prefix[:, :GLA_QK], prefix[:, GLA_QK:]
    tf, tb = total[:, :GLA_QK], total[:, GLA_QK:]
    gb = g[:, GLA_QK:]

    qk = proj(0, 2 * GLA_QK)
    q = qk[:, :GLA_QK] * (GLA_DK ** -0.5)
    k = qk[:, GLA_QK:]
    for cum, rest, o_ref in ((pf, tf - pf, f_ref), (tb - pb + gb, pb - gb, b_ref)):
        o_ref[:, 0:GLA_QK] = (q * jnp.exp(cum)).astype(BF16)
        o_ref[:, GLA_QK:2 * GLA_QK] = (k * jnp.exp(-cum)).astype(BF16)
        o_ref[:, 2 * GLA_QK:3 * GLA_QK] = (k * jnp.exp(rest)).astype(BF16)
    dec = jnp.concatenate([dec] * (DEC_LANES // dec.shape[0]), axis=0)
    decf_ref[0] = dec[:, :GLA_QK].T
    decb_ref[0] = dec[:, GLA_QK:].T


def _chunk_prefix_matrix(rows):
    r = np.arange(rows)[:, None]
    c = np.arange(rows)[None, :]
    return jnp.asarray(((r // GLA_CHUNK) == (c // GLA_CHUNK)) & (c <= r), BF16)


def _rope_tables(seq):
    inv = ROPE_THETA ** (-jnp.arange(0, ROT_DIM, 2, dtype=F32) / ROT_DIM)
    ang = jnp.arange(seq).astype(F32)[:, None] * inv[None, :]
    cos, sin = jnp.cos(ang), jnp.sin(ang)
    rest = DIFF_DH - ROT_DIM
    cos64 = jnp.concatenate([cos, cos, jnp.ones((seq, rest), F32)], axis=1)
    sin64 = jnp.concatenate([-sin, sin, jnp.zeros((seq, rest), F32)], axis=1)
    reps = LANES // DIFF_DH
    return jnp.tile(cos64, (1, reps)), jnp.tile(sin64, (1, reps))


def _in_proj(x2d, seq, w):
    tokens = x2d.shape[0]
    tm = INPROJ_TILE
    tri = _chunk_prefix_matrix(PREFIX_ROWS)
    cos, sin = _rope_tables(seq)
    pos_blocks = seq // tm
    row = lambda i: (i, 0)
    pos = lambda i: (i % pos_blocks, 0)

    def tok(width, dtype=BF16):
        return jax.ShapeDtypeStruct((tokens, width), dtype)

    dec = jax.ShapeDtypeStruct((tokens // INPROJ_ROWS, GLA_QK, DEC_LANES), F32)
    dec_spec = pl.BlockSpec((tm // INPROJ_ROWS, GLA_QK, DEC_LANES), lambda i: (i, 0, 0))
    return pl.pallas_call(
        _inproj_kernel,
        grid=(tokens // tm,),
        in_specs=[pl.BlockSpec((tm, D_MODEL), row),
                  _const_spec((1, D_MODEL)),
                  _const_spec(w["w_a"].shape),
                  _const_spec(w["w_g"].shape),
                  _const_spec(w["w_up"].shape),
                  _const_spec(w["b_g"].shape),
                  _const_spec(tri.shape),
                  pl.BlockSpec((tm, LANES), pos),
                  pl.BlockSpec((tm, LANES), pos)],
        out_specs=[pl.BlockSpec((tm, 3 * GLA_QK), row),
                   pl.BlockSpec((tm, 3 * GLA_QK), row),
                   pl.BlockSpec((tm, GLA_V), row),
                   pl.BlockSpec((tm, GLA_V), row),
                   dec_spec,
                   dec_spec,
                   pl.BlockSpec((tm, DIFF_QK), row),
                   pl.BlockSpec((tm, DIFF_QK), row),
                   pl.BlockSpec((tm, DIFF_V), row)],
        out_shape=[tok(3 * GLA_QK), tok(3 * GLA_QK), tok(GLA_V), tok(GLA_V), dec, dec,
                   tok(DIFF_QK), tok(DIFF_QK), tok(DIFF_V)],
        compiler_params=_params(("parallel",)),
        name="in_proj",
    )(x2d, w["norm_mix_pre"], w["w_a"], w["w_g"], w["w_up"], w["b_g"], tri, cos, sin)


def _gla_kernel(f_ref, b_ref, vf_ref, vb_ref, decf_ref, decb_ref, of_ref, ob_ref, sf_ref, sb_ref):
    nchunk = f_ref.shape[0] // GLA_CHUNK

    @pl.when(pl.program_id(1) == 0)
    def _():
        sf_ref[...] = jnp.zeros_like(sf_ref)
        sb_ref[...] = jnp.zeros_like(sb_ref)

    qk_head = lax.broadcasted_iota(jnp.int32, (GLA_CHUNK, GLA_QK), 1) // GLA_DK
    v_head = lax.broadcasted_iota(jnp.int32, (GLA_CHUNK, GLA_V), 1) // GLA_DV
    s_head = lax.broadcasted_iota(jnp.int32, (GLA_QK, GLA_DV), 0) // GLA_DK
    a_row = lax.broadcasted_iota(jnp.int32, (GLA_CHUNK, GLA_QK), 0)
    a_col = lax.broadcasted_iota(jnp.int32, (GLA_CHUNK, GLA_QK), 1) % GLA_CHUNK

    def chunk_step(x_ref, v_ref, dec_ref, o_ref, s_ref, c, keep):
        rows = slice(c * GLA_CHUNK, (c + 1) * GLA_CHUNK)
        qe = x_ref[rows, 0:GLA_QK]
        ke = x_ref[rows, GLA_QK:2 * GLA_QK]
        kd = x_ref[rows, 2 * GLA_QK:3 * GLA_QK]
        v = v_ref[rows, :]
        per_group = INPROJ_ROWS // GLA_CHUNK
        dec = dec_ref[c // per_group, :, c % per_group:c % per_group + 1]
        ke_heads = jnp.concatenate([jnp.where(qk_head == h, ke, jnp.zeros_like(ke))
                                    for h in range(GLA_HEADS)], axis=0)
        a = jnp.where(keep, _dot_nt(qe, ke_heads), 0.0).astype(BF16)
        v_heads = jnp.concatenate([jnp.where(v_head == h, v, jnp.zeros_like(v))
                                   for h in range(GLA_HEADS)], axis=0)
        state = s_ref[...]
        state16 = state.astype(BF16)
        s_heads = jnp.concatenate([jnp.where(s_head == h, state16, jnp.zeros_like(state16))
                                   for h in range(GLA_HEADS)], axis=1)
        o_ref[rows, :] = (_dot(a, v_heads) + _dot(qe, s_heads)).astype(BF16)
        upd = jnp.concatenate([_dot_tn(kd[:, h * GLA_DK:(h + 1) * GLA_DK],
                                       v[:, h * GLA_DV:(h + 1) * GLA_DV])
                               for h in range(GLA_HEADS)], axis=0)
        s_ref[...] = dec * state + upd

    for c in range(nchunk):
        chunk_step(f_ref, vf_ref, decf_ref, of_ref, sf_ref, c, a_col <= a_row)
        chunk_step(b_ref, vb_ref, decb_ref, ob_ref, sb_ref, nchunk - 1 - c, a_col >= a_row)


def _gla(f, b, v, decf, decb, batch, seq):
    tokens = f.shape[0]
    lb = GLA_BLOCK
    nblk = seq // lb
    nchunk = lb // GLA_CHUNK
    fwd = lambda bi, i: (bi * nblk + i, 0)
    bwd = lambda bi, i: (bi * nblk + nblk - 1 - i, 0)
    fwd3 = lambda bi, i: (bi * nblk + i, 0, 0)
    bwd3 = lambda bi, i: (bi * nblk + nblk - 1 - i, 0, 0)
    out = jax.ShapeDtypeStruct((tokens, GLA_V), BF16)
    return pl.pallas_call(
        _gla_kernel,
        grid=(batch, nblk),
        in_specs=[pl.BlockSpec((lb, 3 * GLA_QK), fwd),
                  pl.BlockSpec((lb, 3 * GLA_QK), bwd),
                  pl.BlockSpec((lb, GLA_V), fwd),
                  pl.BlockSpec((lb, GLA_V), bwd),
                  pl.BlockSpec((lb // INPROJ_ROWS, GLA_QK, DEC_LANES), fwd3),
                  pl.BlockSpec((lb // INPROJ_ROWS, GLA_QK, DEC_LANES), bwd3)],
        out_specs=[pl.BlockSpec((lb, GLA_V), fwd),
                   pl.BlockSpec((lb, GLA_V), bwd)],
        out_shape=[out, out],
        scratch_shapes=[pltpu.VMEM((GLA_QK, GLA_DV), F32),
                        pltpu.VMEM((GLA_QK, GLA_DV), F32)],
        compiler_params=_params(("parallel", "arbitrary")),
        name="gla",
    )(f, b, v, v, decf, decb)


def _diff_kernel(q_ref, k_ref, v_ref, lq1_ref, lk1_ref, lq2_ref, lk2_ref, w_ref, o_ref,
                 qm_ref, m_ref, l_ref, acc_ref, s_ref, *, lam_init):
    ki = pl.program_id(2)
    tq = q_ref.shape[0]
    tk = k_ref.shape[0]
    width = 2 * DIFF_DH

    @pl.when(ki == 0)
    def _():
        m_ref[...] = jnp.full_like(m_ref, -jnp.inf)
        l_ref[...] = jnp.zeros_like(l_ref)
        acc_ref[...] = jnp.zeros_like(acc_ref)
        first = lax.broadcasted_iota(jnp.int32, (tq, width), 1) < DIFF_DH
        for h in range(DIFF_HEADS):
            qh = q_ref[:, h * width:(h + 1) * width]
            zero = jnp.zeros_like(qh)
            qm_ref[2 * h] = jnp.where(first, qh, zero)
            qm_ref[2 * h + 1] = jnp.where(first, zero, qh)

    ones = jnp.ones((ATTN_KEY_BLOCK, width), BF16)
    nstream = 2 * DIFF_HEADS

    def scores(s_idx):
        h = s_idx // 2
        s = _dot_nt(qm_ref[s_idx], k_ref[:, h * width:(h + 1) * width])
        s_ref[s_idx % 2] = s
        m_prev = m_ref[s_idx]
        m_next = jnp.maximum(m_prev, jnp.max(s, axis=1, keepdims=True))
        m_ref[s_idx] = m_next
        return m_prev, m_next

    maxes = scores(0)
    for s_idx in range(nstream):
        m_prev, m_next = maxes
        if s_idx + 1 < nstream:
            maxes = scores(s_idx + 1)
        h = s_idx // 2
        vh = v_ref[:, h * width:(h + 1) * width]
        m_wide = jnp.tile(m_next, (1, ATTN_KEY_BLOCK // LANES))
        pv = jnp.zeros((tq, 2 * width), F32)
        for k0 in range(0, tk, ATTN_KEY_BLOCK):
            keys = slice(k0, k0 + ATTN_KEY_BLOCK)
            pb = jnp.exp2(s_ref[s_idx % 2, :, keys] - m_wide).astype(BF16)
            pv = pv + _dot(pb, jnp.concatenate([vh[keys], ones], axis=1))
        alpha = jnp.exp2(m_prev - m_next)
        acc_ref[s_idx] = alpha * acc_ref[s_idx] + pv[:, :width]
        l_ref[s_idx] = alpha * l_ref[s_idx] + pv[:, width:]

    @pl.when(ki == pl.num_programs(2) - 1)
    def _():
        lam = (jnp.exp(jnp.sum(lq1_ref[...] * lk1_ref[...], axis=1, keepdims=True))
               - jnp.exp(jnp.sum(lq2_ref[...] * lk2_ref[...], axis=1, keepdims=True)) + lam_init)
        for h in range(DIFF_HEADS):
            o1 = acc_ref[2 * h] / l_ref[2 * h]
            o2 = acc_ref[2 * h + 1] / l_ref[2 * h + 1]
            o = _rms(o1 - lam * o2, w_ref[...]) * (1.0 - lam_init)
            o_ref[:, h * width:(h + 1) * width] = o.astype(BF16)


def _diff_attn(dq, dk, dv, w, batch, seq, lam_init):
    tokens = dq.shape[0]
    tq = min(ATTN_Q_TILE, seq)
    tk = min(ATTN_K_TILE, seq)
    nq, nk = seq // tq, seq // tk
    qmap = lambda b, qi, ki: (b * nq + qi, 0)
    kmap = lambda b, qi, ki: (b * nk + ki, 0)
    nstream = 2 * DIFF_HEADS
    vec = _const_spec((1, DIFF_DH))
    return pl.pallas_call(
        functools.partial(_diff_kernel, lam_init=lam_init),
        grid=(batch, nq, nk),
        in_specs=[pl.BlockSpec((tq, DIFF_QK), qmap),
                  pl.BlockSpec((tk, DIFF_QK), kmap),
                  pl.BlockSpec((tk, DIFF_V), kmap),
                  vec, vec, vec, vec,
                  _const_spec((1, 2 * DIFF_DH))],
        out_specs=pl.BlockSpec((tq, DIFF_V), qmap),
        out_shape=jax.ShapeDtypeStruct((tokens, DIFF_V), BF16),
        scratch_shapes=[pltpu.VMEM((nstream, tq, 2 * DIFF_DH), BF16),
                        pltpu.VMEM((nstream, tq, LANES), F32),
                        pltpu.VMEM((nstream, tq, LANES), F32),
                        pltpu.VMEM((nstream, tq, 2 * DIFF_DH), F32),
                        pltpu.VMEM((2, tq, tk), F32)],
        compiler_params=_params(("parallel", "parallel", "arbitrary")),
        name="diff_attn",
    )(dq, dk, dv, w["lambda_q1"], w["lambda_k1"], w["lambda_q2"], w["lambda_k2"], w["diff_subln_w"])


def _mix_xattn_kernel(x_ref, of_ref, ob_ref, og_ref, od_ref, k_ref, v_ref,
                      gnw_ref, wout_ref, npost_ref, nxpre_ref, wxq_ref, wxo_ref, nxpost_ref, y_ref):
    tm = x_ref.shape[0]
    parts = [slice(r, r + tm // MIX_ROW_SPLIT) for r in range(0, tm, tm // MIX_ROW_SPLIT)]

    def gla_out(rows):
        o = of_ref[rows, :].astype(F32) + ob_ref[rows, :].astype(F32)
        og = og_ref[rows, :].astype(F32)
        gate = og * (1.0 / (1.0 + jnp.exp(-og)))
        heads = []
        for h in range(GLA_HEADS):
            cols = slice(h * GLA_DV, (h + 1) * GLA_DV)
            heads.append((_rms(o[:, cols], gnw_ref[...]) * gate[:, cols]).astype(BF16))
        return jnp.concatenate(heads + [od_ref[rows, :]], axis=1)

    def attend(q):
        outs = []
        for h in range(XATTN_HEADS):
            cols = slice(h * XATTN_DH, (h + 1) * XATTN_DH)
            s = _dot_nt(q[:, cols], k_ref[:, cols])
            p = jnp.exp(s - jnp.max(s, axis=1, keepdims=True))
            denom = jnp.sum(p, axis=1, keepdims=True)
            outs.append((_dot(p.astype(BF16), v_ref[:, cols]) / denom).astype(BF16))
        return jnp.concatenate(outs, axis=1)

    mix_in = [gla_out(rows) for rows in parts]
    mix = [_dot(t, wout_ref[...]) for t in mix_in]
    x1 = [x_ref[rows, :] + _rms(t, npost_ref[...]) for rows, t in zip(parts, mix)]
    h2 = [_rms(t, nxpre_ref[...]).astype(BF16) for t in x1]
    q = [(_dot(t, wxq_ref[...]) * (XATTN_DH ** -0.5)).astype(BF16) for t in h2]
    att = [attend(t) for t in q]
    xo = [_dot(t, wxo_ref[...]) for t in att]
    for rows, res, t in zip(parts, x1, xo):
        y_ref[rows, :] = res + _rms(t, nxpost_ref[...])


def _mix_xattn(x2d, o_f, o_b, og, o_diff, mem_k, mem_v, seq, w):
    tokens = x2d.shape[0]
    tm = MIX_ROW_SPLIT * TOKEN_TILE
    per_seq = seq // tm
    row = lambda i: (i, 0)
    memmap = lambda i: (i // per_seq, 0)
    half = pl.BlockSpec((tm, GLA_V), row)
    sq = _const_spec((D_MODEL, D_MODEL))
    nv = _const_spec((1, D_MODEL))
    return pl.pallas_call(
        _mix_xattn_kernel,
        grid=(tokens // tm,),
        in_specs=[pl.BlockSpec((tm, D_MODEL), row), half, half, half, half,
                  pl.BlockSpec((MEM_LEN, D_MODEL), memmap),
                  pl.BlockSpec((MEM_LEN, D_MODEL), memmap),
                  _const_spec((1, GLA_DV)), sq, nv, nv, sq, sq, nv],
        out_specs=pl.BlockSpec((tm, D_MODEL), row),
        out_shape=jax.ShapeDtypeStruct((tokens, D_MODEL), F32),
        compiler_params=_params(("parallel",)),
        name="mix_xattn",
    )(x2d, o_f, o_b, og, o_diff, mem_k, mem_v, w["gla_norm_w"], w["w_out"], w["norm_mix_post"],
      w["norm_xattn_pre"], w["w_xq"], w["w_xo"], w["norm_xattn_post"])


def _ffn_kernel(x_ref, npre_ref, wg_ref, wu_ref, wd_ref, npost_ref, y_ref):
    tm = x_ref.shape[0]
    parts = [slice(r, r + tm // MIX_ROW_SPLIT) for r in range(0, tm, tm // MIX_ROW_SPLIT)]
    h = [_rms(x_ref[rows, :], npre_ref[...]).astype(BF16) for rows in parts]
    g = [_dot(t, wg_ref[...]) for t in h]
    u = [_dot(t, wu_ref[...]) for t in h]
    a = [(tg * (1.0 / (1.0 + jnp.exp(-tg))) * tu).astype(BF16) for tg, tu in zip(g, u)]
    d = [_dot(t, wd_ref[...]) for t in a]
    for rows, t in zip(parts, d):
        y_ref[rows, :] = x_ref[rows, :] + _rms(t, npost_ref[...])


def _ffn(x2d, w):
    tokens = x2d.shape[0]
    tm = MIX_ROW_SPLIT * TOKEN_TILE
    row = lambda i: (i, 0)
    nv = _const_spec((1, D_MODEL))
    return pl.pallas_call(
        _ffn_kernel,
        grid=(tokens // tm,),
        in_specs=[pl.BlockSpec((tm, D_MODEL), row), nv,
                  _const_spec((D_MODEL, D_FF)), _const_spec((D_MODEL, D_FF)),
                  _const_spec((D_FF, D_MODEL)), nv],
        out_specs=pl.BlockSpec((tm, D_MODEL), row),
        out_shape=jax.ShapeDtypeStruct((tokens, D_MODEL), F32),
        compiler_params=_params(("parallel",)),
        name="ffn",
    )(x2d, w["norm_ffn_pre"], w["w_ffn_gate"], w["w_ffn_up"], w["w_ffn_down"], w["norm_ffn_post"])


def _prepare_weights(p):
    w_in = p["w_in"]
    gates_at = 2 * GLA_QK + GLA_V
    gates_end = gates_at + 2 * GLA_GATE_RANK
    w = {}
    w["w_a"] = jnp.concatenate([w_in[:, :gates_at], w_in[:, gates_end:]], axis=1).astype(BF16)
    w["w_g"] = jnp.pad(w_in[:, gates_at:gates_end], ((0, 0), (0, LANES - 2 * GLA_GATE_RANK))).astype(BF16)
    w_up = jnp.zeros((LANES, 2 * GLA_QK), F32)
    w_up = w_up.at[:GLA_GATE_RANK, :GLA_QK].set(p["w_gate_up_f"])
    w_up = w_up.at[GLA_GATE_RANK:2 * GLA_GATE_RANK, GLA_QK:].set(p["w_gate_up_b"])
    w["w_up"] = w_up.astype(BF16)
    w["b_g"] = jnp.concatenate([p["b_gate_f"], p["b_gate_b"]])[None, :]
    for name in ("w_out", "w_xq", "w_xkv", "w_xo", "w_ffn_gate", "w_ffn_up", "w_ffn_down"):
        w[name] = p[name].astype(BF16)
    for name in ("norm_mix_pre", "gla_norm_w", "lambda_q1", "lambda_k1", "lambda_q2", "lambda_k2",
                 "diff_subln_w", "norm_mix_post", "norm_xattn_pre", "norm_mem", "norm_xattn_post",
                 "norm_ffn_pre", "norm_ffn_post"):
        w[name] = p[name][None, :]
    return w


def _encoder_layer(x, mem, w, lam_init):
    batch, seq, _ = x.shape
    assert seq % (MIX_ROW_SPLIT * TOKEN_TILE) == 0 and seq % INPROJ_TILE == 0 and mem.shape[1] == MEM_LEN
    x2d = x.reshape(batch * seq, D_MODEL)
    mem_k, mem_v = _mem_kv(mem.reshape(batch * MEM_LEN, D_MODEL), w["norm_mem"], w["w_xkv"])
    f, b, v, og, decf, decb, dq, dk, dv = _in_proj(x2d, seq, w)
    o_f, o_b = _gla(f, b, v, decf, decb, batch, seq)
    o_diff = _diff_attn(dq, dk, dv, w, batch, seq, lam_init)
    x2 = _mix_xattn(x2d, o_f, o_b, og, o_diff, mem_k, mem_v, seq, w)
    return _ffn(x2, w).reshape(batch, seq, D_MODEL)


def kernel(x_prompt, x_sample, mem_prompt, mem_sample, norm_mix_pre, w_in, w_gate_up_f, b_gate_f,
           w_gate_up_b, b_gate_b, gla_norm_w, lambda_q1, lambda_k1, lambda_q2, lambda_k2, diff_subln_w,
           w_out, norm_mix_post, norm_xattn_pre, norm_mem, w_xq, w_xkv, w_xo, norm_xattn_post,
           norm_ffn_pre, w_ffn_gate, w_ffn_up, w_ffn_down, norm_ffn_post):
    stacked = dict(
        norm_mix_pre=norm_mix_pre, w_in=w_in, w_gate_up_f=w_gate_up_f, b_gate_f=b_gate_f,
        w_gate_up_b=w_gate_up_b, b_gate_b=b_gate_b, gla_norm_w=gla_norm_w, lambda_q1=lambda_q1,
        lambda_k1=lambda_k1, lambda_q2=lambda_q2, lambda_k2=lambda_k2, diff_subln_w=diff_subln_w,
        w_out=w_out, norm_mix_post=norm_mix_post, norm_xattn_pre=norm_xattn_pre, norm_mem=norm_mem,
        w_xq=w_xq, w_xkv=w_xkv, w_xo=w_xo, norm_xattn_post=norm_xattn_post,
        norm_ffn_pre=norm_ffn_pre, w_ffn_gate=w_ffn_gate, w_ffn_up=w_ffn_up, w_ffn_down=w_ffn_down,
        norm_ffn_post=norm_ffn_post)
    y_prompt, y_sample = x_prompt, x_sample
    for layer in range(w_in.shape[0]):
        lam_init = 0.8 - 0.6 * math.exp(-0.3 * layer)
        w = _prepare_weights({name: t[layer] for name, t in stacked.items()})
        y_prompt = _encoder_layer(y_prompt, mem_prompt, w, lam_init)
        y_sample = _encoder_layer(y_sample, mem_sample, w, lam_init)
    return (y_prompt, y_sample)
```

```python
import functools
import math

import numpy as np
import jax
import jax.numpy as jnp
from jax import lax
from jax.experimental import pallas as pl
from jax.experimental.pallas import tpu as pltpu

F32 = jnp.float32
BF16 = jnp.bfloat16

D_MODEL = 1024
GLA_HEADS = 4
GLA_DK = 64
GLA_DV = 128
GLA_QK = GLA_HEADS * GLA_DK
GLA_V = GLA_HEADS * GLA_DV
GLA_GATE_RANK = 16
GLA_GATE_NORMALIZER = 16.0
GLA_CHUNK = 64
DIFF_HEADS = 4
DIFF_DH = 64
DIFF_QK = DIFF_HEADS * 2 * DIFF_DH
DIFF_V = DIFF_HEADS * 2 * DIFF_DH
ROT_DIM = DIFF_DH // 4
ROPE_THETA = 500000.0
MEM_LEN = 256
XATTN_HEADS = 4
XATTN_DH = D_MODEL // XATTN_HEADS
D_FF = 2816
RMS_EPS = 1e-6

LANES = 128
TOKEN_TILE = 256
GLA_BLOCK = 1024
INPROJ_TILE = 1024
PREFIX_ROWS = 256
ATTN_Q_TILE = 512
ATTN_K_TILE = 2048
MIX_ROW_SPLIT = 4
ATTN_KEY_BLOCK = 256
LOG2E = math.log2(math.e)
VMEM_LIMIT = 56 * 1024 * 1024

NT_DIMS = (((1,), (1,)), ((), ()))
TN_DIMS = (((0,), (0,)), ((), ()))


def _rms(x, w):
    return x * lax.rsqrt(jnp.mean(x * x, axis=-1, keepdims=True) + RMS_EPS) * w


def _dot(a, b):
    return jnp.dot(a, b, preferred_element_type=F32)


def _dot_nt(a, b):
    return lax.dot_general(a, b, NT_DIMS, preferred_element_type=F32)


def _dot_tn(a, b):
    return lax.dot_general(a, b, TN_DIMS, preferred_element_type=F32)


def _const_spec(shape):
    return pl.BlockSpec(shape, lambda *_: (0,) * len(shape), pipeline_mode=pl.Buffered(1))


def _params(semantics):
    return pltpu.CompilerParams(dimension_semantics=semantics, vmem_limit_bytes=VMEM_LIMIT)


def _mem_kv_kernel(m_ref, nw_ref, w_ref, k_ref, v_ref):
    m = _rms(m_ref[...], nw_ref[...]).astype(BF16)
    kv = _dot(m, w_ref[...])
    k_ref[...] = kv[:, :D_MODEL].astype(BF16)
    v_ref[...] = kv[:, D_MODEL:].astype(BF16)


def _mem_kv(mem2d, norm_w, w_xkv):
    rows = mem2d.shape[0]
    out = jax.ShapeDtypeStruct((rows, D_MODEL), BF16)
    return pl.pallas_call(
        _mem_kv_kernel,
        grid=(rows // MEM_LEN,),
        in_specs=[pl.BlockSpec((MEM_LEN, D_MODEL), lambda i: (i, 0)),
                  _const_spec((1, D_MODEL)),
                  _const_spec((D_MODEL, 2 * D_MODEL))],
        out_specs=[pl.BlockSpec((MEM_LEN, D_MODEL), lambda i: (i, 0))] * 2,
        out_shape=[out, out],
        compiler_params=_params(("parallel",)),
        name="mem_kv",
    )(mem2d, norm_w, w_xkv)


def _inproj_kernel(x_ref, nw_ref, wa_ref, wg_ref, wup_ref, bg_ref, tri_ref, cos_ref, sin_ref,
                   f_ref, b_ref, v_ref, og_ref, decf_ref, decb_ref, dq_ref, dk_ref, dv_ref):
    for j in range(x_ref.shape[0] // GLA_BLOCK):
        rows = pl.ds(j * GLA_BLOCK, GLA_BLOCK)
        blk = pl.ds(j, 1)
        _inproj_rows(x_ref.at[rows], nw_ref, wa_ref, wg_ref, wup_ref, bg_ref, tri_ref,
                     cos_ref.at[rows], sin_ref.at[rows], f_ref.at[rows], b_ref.at[rows],
                     v_ref.at[rows], og_ref.at[rows], decf_ref.at[blk], decb_ref.at[blk],
                     dq_ref.at[rows], dk_ref.at[rows], dv_ref.at[rows])


def _inproj_rows(x_ref, nw_ref, wa_ref, wg_ref, wup_ref, bg_ref, tri_ref, cos_ref, sin_ref,
                 f_ref, b_ref, v_ref, og_ref, decf_ref, decb_ref, dq_ref, dk_ref, dv_ref):
    tm = x_ref.shape[0]
    h = _rms(x_ref[...], nw_ref[...]).astype(BF16)

    def proj(off, width):
        return _dot(h, wa_ref[:, off:off + width])

    gdown = _dot(h, wg_ref[...]).astype(BF16)
    pre = _dot(gdown, wup_ref[...]) + bg_ref[...]

    off = 2 * GLA_QK
    v_ref[...] = proj(off, GLA_V).astype(BF16)
    off += GLA_V
    og_ref[...] = proj(off, GLA_V).astype(BF16)
    off += GLA_V

    cos = cos_ref[...]
    sin = sin_ref[...]
    lane = lax.broadcasted_iota(jnp.int32, (tm, LANES), 1)
    second_half = (lane & (DIFF_DH - 1)) >= (ROT_DIM // 2)

    def rope(t):
        slabs = []
        for s in range(t.shape[1] // LANES):
            ts = t[:, s * LANES:(s + 1) * LANES]
            partner = jnp.where(second_half, pltpu.roll(ts, ROT_DIM // 2, 1),
                                pltpu.roll(ts, LANES - ROT_DIM // 2, 1))
            slabs.append(ts * cos + partner * sin)
        return jnp.concatenate(slabs, axis=1)

    dq = proj(off, DIFF_QK)
    dq_ref[...] = (rope(dq) * (DIFF_DH ** -0.5 * LOG2E)).astype(BF16)
    off += DIFF_QK
    dk = proj(off, DIFF_QK)
    dk_ref[...] = rope(dk).astype(BF16)
    off += DIFF_QK
    dv_ref[...] = proj(off, DIFF_V).astype(BF16)

    g = (jnp.minimum(pre, 0.0) - jnp.log1p(jnp.exp(-jnp.abs(pre)))) * (1.0 / GLA_GATE_NORMALIZER)
    g_hi = g.astype(BF16)
    g_lo = (g - g_hi.astype(F32)).astype(BF16)
    g_split = jnp.concatenate([g_hi, g_lo], axis=1)
    tri = tri_ref[...]
    half = tri.shape[0]
    pre_sum = jnp.concatenate([_dot(tri, g_split[r:r + half]) for r in range(0, tm, half)], axis=0)
    prefix = pre_sum[:, :2 * GLA_QK] + pre_sum[:, 2 * GLA_QK:]
    totals = [prefix[c + GLA_CHUNK - 1:c + GLA_CHUNK] for c in range(0, tm, GLA_CHUNK)]
    total = jnp.concatenate([jnp.broadcast_to(t, (GLA_CHUNK, 2 * GLA_QK)) for t in totals], axis=0)
    dec = jnp.exp(jnp.concatenate(totals, axis=0))
    pf, pb = prefix[:, :GLA_QK], prefix[:, GLA_QK:]
    tf, tb = total[:, :GLA_QK], total[:, GLA_QK:]
    gb = g[:, GLA_QK:]

    qk = proj(0, 2 * GLA_QK)
    q = qk[:, :GLA_QK] * (GLA_DK ** -0.5)
    k = qk[:, GLA_QK:]
    for cum, rest, o_ref in ((pf, tf - pf, f_ref), (tb - pb + gb, pb - gb, b_ref)):
        o_ref[:, 0:GLA_QK] = (q * jnp.exp(cum)).astype(BF16)
        o_ref[:, GLA_QK:2 * GLA_QK] = (k * jnp.exp(-cum)).astype(BF16)
        o_ref[:, 2 * GLA_QK:3 * GLA_QK] = (k * jnp.exp(rest)).astype(BF16)
    decf_ref[0] = dec[:, :GLA_QK].T
    decb_ref[0] = dec[:, GLA_QK:].T


def _chunk_prefix_matrix(rows):
    r = np.arange(rows)[:, None]
    c = np.arange(rows)[None, :]
    return jnp.asarray(((r // GLA_CHUNK) == (c // GLA_CHUNK)) & (c <= r), BF16)


def _rope_tables(seq):
    inv = ROPE_THETA ** (-jnp.arange(0, ROT_DIM, 2, dtype=F32) / ROT_DIM)
    ang = jnp.arange(seq).astype(F32)[:, None] * inv[None, :]
    cos, sin = jnp.cos(ang), jnp.sin(ang)
    rest = DIFF_DH - ROT_DIM
    cos64 = jnp.concatenate([cos, cos, jnp.ones((seq, rest), F32)], axis=1)
    sin64 = jnp.concatenate([-sin, sin, jnp.zeros((seq, rest), F32)], axis=1)
    reps = LANES // DIFF_DH
    return jnp.tile(cos64, (1, reps)), jnp.tile(sin64, (1, reps))


def _in_proj(x2d, seq, w):
    tokens = x2d.shape[0]
    tm = INPROJ_TILE
    nchunk = GLA_BLOCK // GLA_CHUNK
    tri = _chunk_prefix_matrix(PREFIX_ROWS)
    cos, sin = _rope_tables(seq)
    pos_blocks = seq // tm
    row = lambda i: (i, 0)
    pos = lambda i: (i % pos_blocks, 0)

    def tok(width, dtype=BF16):
        return jax.ShapeDtypeStruct((tokens, width), dtype)

    dec = jax.ShapeDtypeStruct((tokens // GLA_BLOCK, GLA_QK, nchunk), F32)
    dec_spec = pl.BlockSpec((tm // GLA_BLOCK, GLA_QK, nchunk), lambda i: (i, 0, 0))
    return pl.pallas_call(
        _inproj_kernel,
        grid=(tokens // tm,),
        in_specs=[pl.BlockSpec((tm, D_MODEL), row),
                  _const_spec((1, D_MODEL)),
                  _const_spec(w["w_a"].shape),
                  _const_spec(w["w_g"].shape),
                  _const_spec(w["w_up"].shape),
                  _const_spec(w["b_g"].shape),
                  _const_spec(tri.shape),
                  pl.BlockSpec((tm, LANES), pos),
                  pl.BlockSpec((tm, LANES), pos)],
        out_specs=[pl.BlockSpec((tm, 3 * GLA_QK), row),
                   pl.BlockSpec((tm, 3 * GLA_QK), row),
                   pl.BlockSpec((tm, GLA_V), row),
                   pl.BlockSpec((tm, GLA_V), row),
                   dec_spec,
                   dec_spec,
                   pl.BlockSpec((tm, DIFF_QK), row),
                   pl.BlockSpec((tm, DIFF_QK), row),
                   pl.BlockSpec((tm, DIFF_V), row)],
        out_shape=[tok(3 * GLA_QK), tok(3 * GLA_QK), tok(GLA_V), tok(GLA_V), dec, dec,
                   tok(DIFF_QK), tok(DIFF_QK), tok(DIFF_V)],
        compiler_params=_params(("parallel",)),
        name="in_proj",
    )(x2d, w["norm_mix_pre"], w["w_a"], w["w_g"], w["w_up"], w["b_g"], tri, cos, sin)


def _gla_kernel(f_ref, b_ref, vf_ref, vb_ref, decf_ref, decb_ref, of_ref, ob_ref, sf_ref, sb_ref):
    nchunk = f_ref.shape[0] // GLA_CHUNK

    @pl.when(pl.program_id(1) == 0)
    def _():
        sf_ref[...] = jnp.zeros_like(sf_ref)
        sb_ref[...] = jnp.zeros_like(sb_ref)

    qk_head = lax.broadcasted_iota(jnp.int32, (GLA_CHUNK, GLA_QK), 1) // GLA_DK
    v_head = lax.broadcasted_iota(jnp.int32, (GLA_CHUNK, GLA_V), 1) // GLA_DV
    s_head = lax.broadcasted_iota(jnp.int32, (GLA_QK, GLA_DV), 0) // GLA_DK
    a_row = lax.broadcasted_iota(jnp.int32, (GLA_CHUNK, GLA_QK), 0)
    a_col = lax.broadcasted_iota(jnp.int32, (GLA_CHUNK, GLA_QK), 1) % GLA_CHUNK

    def chunk_step(x_ref, v_ref, dec_ref, o_ref, s_ref, c, keep):
        rows = slice(c * GLA_CHUNK, (c + 1) * GLA_CHUNK)
        qe = x_ref[rows, 0:GLA_QK]
        ke = x_ref[rows, GLA_QK:2 * GLA_QK]
        kd = x_ref[rows, 2 * GLA_QK:3 * GLA_QK]
        v = v_ref[rows, :]
        dec = dec_ref[0, :, c:c + 1]
        ke_heads = jnp.concatenate([jnp.where(qk_head == h, ke, jnp.zeros_like(ke))
                                    for h in range(GLA_HEADS)], axis=0)
        a = jnp.where(keep, _dot_nt(qe, ke_heads), 0.0).astype(BF16)
        v_heads = jnp.concatenate([jnp.where(v_head == h, v, jnp.zeros_like(v))
                                   for h in range(GLA_HEADS)], axis=0)
        state = s_ref[...]
        state16 = state.astype(BF16)
        s_heads = jnp.concatenate([jnp.where(s_head == h, state16, jnp.zeros_like(state16))
                                   for h in range(GLA_HEADS)], axis=1)
        o_ref[rows, :] = (_dot(a, v_heads) + _dot(qe, s_heads)).astype(BF16)
        upd = jnp.concatenate([_dot_tn(kd[:, h * GLA_DK:(h + 1) * GLA_DK],
                                       v[:, h * GLA_DV:(h + 1) * GLA_DV])
                               for h in range(GLA_HEADS)], axis=0)
        s_ref[...] = dec * state + upd

    for c in range(nchunk):
        chunk_step(f_ref, vf_ref, decf_ref, of_ref, sf_ref, c, a_col <= a_row)
        chunk_step(b_ref, vb_ref, decb_ref, ob_ref, sb_ref, nchunk - 1 - c, a_col >= a_row)


def _gla(f, b, v, decf, decb, batch, seq):
    tokens = f.shape[0]
    lb = GLA_BLOCK
    nblk = seq // lb
    nchunk = lb // GLA_CHUNK
    fwd = lambda bi, i: (bi * nblk + i, 0)
    bwd = lambda bi, i: (bi * nblk + nblk - 1 - i, 0)
    fwd3 = lambda bi, i: (bi * nblk + i, 0, 0)
    bwd3 = lambda bi, i: (bi * nblk + nblk - 1 - i, 0, 0)
    out = jax.ShapeDtypeStruct((tokens, GLA_V), BF16)
    return pl.pallas_call(
        _gla_kernel,
        grid=(batch, nblk),
        in_specs=[pl.BlockSpec((lb, 3 * GLA_QK), fwd),
                  pl.BlockSpec((lb, 3 * GLA_QK), bwd),
                  pl.BlockSpec((lb, GLA_V), fwd),
                  pl.BlockSpec((lb, GLA_V), bwd),
                  pl.BlockSpec((1, GLA_QK, nchunk), fwd3),
                  pl.BlockSpec((1, GLA_QK, nchunk), bwd3)],
        out_specs=[pl.BlockSpec((lb, GLA_V), fwd),
                   pl.BlockSpec((lb, GLA_V), bwd)],
        out_shape=[out, out],
        scratch_shapes=[pltpu.VMEM((GLA_QK, GLA_DV), F32),
                        pltpu.VMEM((GLA_QK, GLA_DV), F32)],
        compiler_params=_params(("parallel", "arbitrary")),
        name="gla",
    )(f, b, v, v, decf, decb)


def _diff_kernel(q_ref, k_ref, v_ref, lq1_ref, lk1_ref, lq2_ref, lk2_ref, w_ref, o_ref,
                 qm_ref, m_ref, l_ref, acc_ref, s_ref, *, lam_init):
    ki = pl.program_id(2)
    tq = q_ref.shape[0]
    tk = k_ref.shape[0]
    width = 2 * DIFF_DH

    @pl.when(ki == 0)
    def _():
        m_ref[...] = jnp.full_like(m_ref, -jnp.inf)
        l_ref[...] = jnp.zeros_like(l_ref)
        acc_ref[...] = jnp.zeros_like(acc_ref)
        first = lax.broadcasted_iota(jnp.int32, (tq, width), 1) < DIFF_DH
        for h in range(DIFF_HEADS):
            qh = q_ref[:, h * width:(h + 1) * width]
            zero = jnp.zeros_like(qh)
            qm_ref[2 * h] = jnp.where(first, qh, zero)
            qm_ref[2 * h + 1] = jnp.where(first, zero, qh)

    ones = jnp.ones((ATTN_KEY_BLOCK, width), BF16)
    nstream = 2 * DIFF_HEADS

    def scores(s_idx):
        h = s_idx // 2
        s = _dot_nt(qm_ref[s_idx], k_ref[:, h * width:(h + 1) * width])
        s_ref[s_idx % 2] = s
        m_prev = m_ref[s_idx]
        m_next = jnp.maximum(m_prev, jnp.max(s, axis=1, keepdims=True))
        m_ref[s_idx] = m_next
        return m_prev, m_next

    maxes = scores(0)
    for s_idx in range(nstream):
        m_prev, m_next = maxes
        if s_idx + 1 < nstream:
            maxes = scores(s_idx + 1)
        h = s_idx // 2
        vh = v_ref[:, h * width:(h + 1) * width]
        m_wide = jnp.tile(m_next, (1, ATTN_KEY_BLOCK // LANES))
        pv = jnp.zeros((tq, 2 * width), F32)
        for k0 in range(0, tk, ATTN_KEY_BLOCK):
            keys = slice(k0, k0 + ATTN_KEY_BLOCK)
            pb = jnp.exp2(s_ref[s_idx % 2, :, keys] - m_wide).astype(BF16)
            pv = pv + _dot(pb, jnp.concatenate([vh[keys], ones], axis=1))
        alpha = jnp.exp2(m_prev - m_next)
        acc_ref[s_idx] = alpha * acc_ref[s_idx] + pv[:, :width]
        l_ref[s_idx] = alpha * l_ref[s_idx] + pv[:, width:]

    @pl.when(ki == pl.num_programs(2) - 1)
    def _():
        lam = (jnp.exp(jnp.sum(lq1_ref[...] * lk1_ref[...], axis=1, keepdims=True))
               - jnp.exp(jnp.sum(lq2_ref[...] * lk2_ref[...], axis=1, keepdims=True)) + lam_init)
        for h in range(DIFF_HEADS):
            o1 = acc_ref[2 * h] / l_ref[2 * h]
            o2 = acc_ref[2 * h + 1] / l_ref[2 * h + 1]
            o = _rms(o1 - lam * o2, w_ref[...]) * (1.0 - lam_init)
            o_ref[:, h * width:(h + 1) * width] = o.astype(BF16)


def _diff_attn(dq, dk, dv, w, batch, seq, lam_init):
    tokens = dq.shape[0]
    tq = min(ATTN_Q_TILE, seq)
    tk = min(ATTN_K_TILE, seq)
    nq, nk = seq // tq, seq // tk
    qmap = lambda b, qi, ki: (b * nq + qi, 0)
    kmap = lambda b, qi, ki: (b * nk + ki, 0)
    nstream = 2 * DIFF_HEADS
    vec = _const_spec((1, DIFF_DH))
    return pl.pallas_call(
        functools.partial(_diff_kernel, lam_init=lam_init),
        grid=(batch, nq, nk),
        in_specs=[pl.BlockSpec((tq, DIFF_QK), qmap),
                  pl.BlockSpec((tk, DIFF_QK), kmap),
                  pl.BlockSpec((tk, DIFF_V), kmap),
                  vec, vec, vec, vec,
                  _const_spec((1, 2 * DIFF_DH))],
        out_specs=pl.BlockSpec((tq, DIFF_V), qmap),
        out_shape=jax.ShapeDtypeStruct((tokens, DIFF_V), BF16),
        scratch_shapes=[pltpu.VMEM((nstream, tq, 2 * DIFF_DH), BF16),
                        pltpu.VMEM((nstream, tq, LANES), F32),
                        pltpu.VMEM((nstream, tq, LANES), F32),
                        pltpu.VMEM((nstream, tq, 2 * DIFF_DH), F32),
                        pltpu.VMEM((2, tq, tk), F32)],
        compiler_params=_params(("parallel", "parallel", "arbitrary")),
        name="diff_attn",
    )(dq, dk, dv, w["lambda_q1"], w["lambda_k1"], w["lambda_q2"], w["lambda_k2"], w["diff_subln_w"])


def _mix_xattn_kernel(x_ref, of_ref, ob_ref, og_ref, od_ref, k_ref, v_ref,
                      gnw_ref, wout_ref, npost_ref, nxpre_ref, wxq_ref, wxo_ref, nxpost_ref, y_ref):
    tm = x_ref.shape[0]
    parts = [slice(r, r + tm // MIX_ROW_SPLIT) for r in range(0, tm, tm // MIX_ROW_SPLIT)]

    def gla_out(rows):
        o = of_ref[rows, :].astype(F32) + ob_ref[rows, :].astype(F32)
        og = og_ref[rows, :].astype(F32)
        gate = og * (1.0 / (1.0 + jnp.exp(-og)))
        heads = []
        for h in range(GLA_HEADS):
            cols = slice(h * GLA_DV, (h + 1) * GLA_DV)
            heads.append((_rms(o[:, cols], gnw_ref[...]) * gate[:, cols]).astype(BF16))
        return jnp.concatenate(heads + [od_ref[rows, :]], axis=1)

    def attend(q):
        outs = []
        for h in range(XATTN_HEADS):
            cols = slice(h * XATTN_DH, (h + 1) * XATTN_DH)
            s = _dot_nt(q[:, cols], k_ref[:, cols])
            p = jnp.exp(s - jnp.max(s, axis=1, keepdims=True))
            denom = jnp.sum(p, axis=1, keepdims=True)
            outs.append((_dot(p.astype(BF16), v_ref[:, cols]) / denom).astype(BF16))
        return jnp.concatenate(outs, axis=1)

    mix_in = [gla_out(rows) for rows in parts]
    mix = [_dot(t, wout_ref[...]) for t in mix_in]
    x1 = [x_ref[rows, :] + _rms(t, npost_ref[...]) for rows, t in zip(parts, mix)]
    h2 = [_rms(t, nxpre_ref[...]).astype(BF16) for t in x1]
    q = [(_dot(t, wxq_ref[...]) * (XATTN_DH ** -0.5)).astype(BF16) for t in h2]
    att = [attend(t) for t in q]
    xo = [_dot(t, wxo_ref[...]) for t in att]
    for rows, res, t in zip(parts, x1, xo):
        y_ref[rows, :] = res + _rms(t, nxpost_ref[...])


def _mix_xattn(x2d, o_f, o_b, og, o_diff, mem_k, mem_v, seq, w):
    tokens = x2d.shape[0]
    tm = MIX_ROW_SPLIT * TOKEN_TILE
    per_seq = seq // tm
    row = lambda i: (i, 0)
    memmap = lambda i: (i // per_seq, 0)
    half = pl.BlockSpec((tm, GLA_V), row)
    sq = _const_spec((D_MODEL, D_MODEL))
    nv = _const_spec((1, D_MODEL))
    return pl.pallas_call(
        _mix_xattn_kernel,
        grid=(tokens // tm,),
        in_specs=[pl.BlockSpec((tm, D_MODEL), row), half, half, half, half,
                  pl.BlockSpec((MEM_LEN, D_MODEL), memmap),
                  pl.BlockSpec((MEM_LEN, D_MODEL), memmap),
                  _const_spec((1, GLA_DV)), sq, nv, nv, sq, sq, nv],
        out_specs=pl.BlockSpec((tm, D_MODEL), row),
        out_shape=jax.ShapeDtypeStruct((tokens, D_MODEL), F32),
        compiler_params=_params(("parallel",)),
        name="mix_xattn",
    )(x2d, o_f, o_b, og, o_diff, mem_k, mem_v, w["gla_norm_w"], w["w_out"], w["norm_mix_post"],
      w["norm_xattn_pre"], w["w_xq"], w["w_xo"], w["norm_xattn_post"])


def _ffn_kernel(x_ref, npre_ref, wg_ref, wu_ref, wd_ref, npost_ref, y_ref):
    tm = x_ref.shape[0]
    parts = [slice(r, r + tm // MIX_ROW_SPLIT) for r in range(0, tm, tm // MIX_ROW_SPLIT)]
    h = [_rms(x_ref[rows, :], npre_ref[...]).astype(BF16) for rows in parts]
    g = [_dot(t, wg_ref[...]) for t in h]
    u = [_dot(t, wu_ref[...]) for t in h]
    a = [(tg * (1.0 / (1.0 + jnp.exp(-tg))) * tu).astype(BF16) for tg, tu in zip(g, u)]
    d = [_dot(t, wd_ref[...]) for t in a]
    for rows, t in zip(parts, d):
        y_ref[rows, :] = x_ref[rows, :] + _rms(t, npost_ref[...])


def _ffn(x2d, w):
    tokens = x2d.shape[0]
    tm = MIX_ROW_SPLIT * TOKEN_TILE
    row = lambda i: (i, 0)
    nv = _const_spec((1, D_MODEL))
    return pl.pallas_call(
        _ffn_kernel,
        grid=(tokens // tm,),
        in_specs=[pl.BlockSpec((tm, D_MODEL), row), nv,
                  _const_spec((D_MODEL, D_FF)), _const_spec((D_MODEL, D_FF)),
                  _const_spec((D_FF, D_MODEL)), nv],
        out_specs=pl.BlockSpec((tm, D_MODEL), row),
        out_shape=jax.ShapeDtypeStruct((tokens, D_MODEL), F32),
        compiler_params=_params(("parallel",)),
        name="ffn",
    )(x2d, w["norm_ffn_pre"], w["w_ffn_gate"], w["w_ffn_up"], w["w_ffn_down"], w["norm_ffn_post"])


def _prepare_weights(p):
    w_in = p["w_in"]
    gates_at = 2 * GLA_QK + GLA_V
    gates_end = gates_at + 2 * GLA_GATE_RANK
    w = {}
    w["w_a"] = jnp.concatenate([w_in[:, :gates_at], w_in[:, gates_end:]], axis=1).astype(BF16)
    w["w_g"] = jnp.pad(w_in[:, gates_at:gates_end], ((0, 0), (0, LANES - 2 * GLA_GATE_RANK))).astype(BF16)
    w_up = jnp.zeros((LANES, 2 * GLA_QK), F32)
    w_up = w_up.at[:GLA_GATE_RANK, :GLA_QK].set(p["w_gate_up_f"])
    w_up = w_up.at[GLA_GATE_RANK:2 * GLA_GATE_RANK, GLA_QK:].set(p["w_gate_up_b"])
    w["w_up"] = w_up.astype(BF16)
    w["b_g"] = jnp.concatenate([p["b_gate_f"], p["b_gate_b"]])[None, :]
    for name in ("w_out", "w_xq", "w_xkv", "w_xo", "w_ffn_gate", "w_ffn_up", "w_ffn_down"):
        w[name] = p[name].astype(BF16)
    for name in ("norm_mix_pre", "gla_norm_w", "lambda_q1", "lambda_k1", "lambda_q2", "lambda_k2",
                 "diff_subln_w", "norm_mix_post", "norm_xattn_pre", "norm_mem", "norm_xattn_post",
                 "norm_ffn_pre", "norm_ffn_post"):
        w[name] = p[name][None, :]
    return w


def _encoder_layer(x, mem, w, lam_init):
    batch, seq, _ = x.shape
    assert seq % (MIX_ROW_SPLIT * TOKEN_TILE) == 0 and seq % INPROJ_TILE == 0 and mem.shape[1] == MEM_LEN
    x2d = x.reshape(batch * seq, D_MODEL)
    mem_k, mem_v = _mem_kv(mem.reshape(batch * MEM_LEN, D_MODEL), w["norm_mem"], w["w_xkv"])
    f, b, v, og, decf, decb, dq, dk, dv = _in_proj(x2d, seq, w)
    o_f, o_b = _gla(f, b, v, decf, decb, batch, seq)
    o_diff = _diff_attn(dq, dk, dv, w, batch, seq, lam_init)
    x2 = _mix_xattn(x2d, o_f, o_b, og, o_diff, mem_k, mem_v, seq, w)
    return _ffn(x2, w).reshape(batch, seq, D_MODEL)


def kernel(x_prompt, x_sample, mem_prompt, mem_sample, norm_mix_pre, w_in, w_gate_up_f, b_gate_f,
           w_gate_up_b, b_gate_b, gla_norm_w, lambda_q1, lambda_k1, lambda_q2, lambda_k2, diff_subln_w,
           w_out, norm_mix_post, norm_xattn_pre, norm_mem, w_xq, w_xkv, w_xo, norm_xattn_post,
           norm_ffn_pre, w_ffn_gate, w_ffn_up, w_ffn_down, norm_ffn_post):
    stacked = dict(
        norm_mix_pre=norm_mix_pre, w_in=w_in, w_gate_up_f=w_gate_up_f, b_gate_f=b_gate_f,
        w_gate_up_b=w_gate_up_b, b_gate_b=b_gate_b, gla_norm_w=gla_norm_w, lambda_q1=lambda_q1,
        lambda_k1=lambda_k1, lambda_q2=lambda_q2, lambda_k2=lambda_k2, diff_subln_w=diff_subln_w,
        w_out=w_out, norm_mix_post=norm_mix_post, norm_xattn_pre=norm_xattn_pre, norm_mem=norm_mem,
        w_xq=w_xq, w_xkv=w_xkv, w_xo=w_xo, norm_xattn_post=norm_xattn_post,
        norm_ffn_pre=norm_ffn_pre, w_ffn_gate=w_ffn_gate, w_ffn_up=w_ffn_up, w_ffn_down=w_ffn_down,
        norm_ffn_post=norm_ffn_post)
    y_prompt, y_sample = x_prompt, x_sample
    for layer in range(w_in.shape[0]):
        lam_init = 0.8 - 0.6 * math.exp(-0.3 * layer)
        w = _prepare_weights({name: t[layer] for name, t in stacked.items()})
        y_prompt = _encoder_layer(y_prompt, mem_prompt, w, lam_init)
        y_sample = _encoder_layer(y_sample, mem_sample, w, lam_init)
    return (y_prompt, y_sample)
```

```python
import functools
import math

import numpy as np
import jax
import jax.numpy as jnp
from jax import lax
from jax.experimental import pallas as pl
from jax.experimental.pallas import tpu as pltpu

F32 = jnp.float32
BF16 = jnp.bfloat16

D_MODEL = 1024
GLA_HEADS = 4
GLA_DK = 64
GLA_DV = 128
GLA_QK = GLA_HEADS * GLA_DK
GLA_V = GLA_HEADS * GLA_DV
GLA_GATE_RANK = 16
GLA_GATE_NORMALIZER = 16.0
GLA_CHUNK = 64
DIFF_HEADS = 4
DIFF_DH = 64
DIFF_QK = DIFF_HEADS * 2 * DIFF_DH
DIFF_V = DIFF_HEADS * 2 * DIFF_DH
ROT_DIM = DIFF_DH // 4
ROPE_THETA = 500000.0
MEM_LEN = 256
XATTN_HEADS = 4
XATTN_DH = D_MODEL // XATTN_HEADS
D_FF = 2816
RMS_EPS = 1e-6

LANES = 128
TOKEN_TILE = 256
GLA_BLOCK = 1024
INPROJ_TILE = 1024
INPROJ_ROWS = 512
PREFIX_ROWS = 256
ATTN_Q_TILE = 512
ATTN_K_TILE = 2048
MIX_ROW_SPLIT = 4
ATTN_KEY_BLOCK = 256
LOG2E = math.log2(math.e)
VMEM_LIMIT = 56 * 1024 * 1024

NT_DIMS = (((1,), (1,)), ((), ()))
TN_DIMS = (((0,), (0,)), ((), ()))


def _rms(x, w):
    return x * lax.rsqrt(jnp.mean(x * x, axis=-1, keepdims=True) + RMS_EPS) * w


def _dot(a, b):
    return jnp.dot(a, b, preferred_element_type=F32)


def _dot_nt(a, b):
    return lax.dot_general(a, b, NT_DIMS, preferred_element_type=F32)


def _dot_tn(a, b):
    return lax.dot_general(a, b, TN_DIMS, preferred_element_type=F32)


def _const_spec(shape):
    return pl.BlockSpec(shape, lambda *_: (0,) * len(shape), pipeline_mode=pl.Buffered(1))


def _params(semantics):
    return pltpu.CompilerParams(dimension_semantics=semantics, vmem_limit_bytes=VMEM_LIMIT)


def _mem_kv_kernel(m_ref, nw_ref, w_ref, k_ref, v_ref):
    m = _rms(m_ref[...], nw_ref[...]).astype(BF16)
    kv = _dot(m, w_ref[...])
    k_ref[...] = kv[:, :D_MODEL].astype(BF16)
    v_ref[...] = kv[:, D_MODEL:].astype(BF16)


def _mem_kv(mem2d, norm_w, w_xkv):
    rows = mem2d.shape[0]
    out = jax.ShapeDtypeStruct((rows, D_MODEL), BF16)
    return pl.pallas_call(
        _mem_kv_kernel,
        grid=(rows // MEM_LEN,),
        in_specs=[pl.BlockSpec((MEM_LEN, D_MODEL), lambda i: (i, 0)),
                  _const_spec((1, D_MODEL)),
                  _const_spec((D_MODEL, 2 * D_MODEL))],
        out_specs=[pl.BlockSpec((MEM_LEN, D_MODEL), lambda i: (i, 0))] * 2,
        out_shape=[out, out],
        compiler_params=_params(("parallel",)),
        name="mem_kv",
    )(mem2d, norm_w, w_xkv)


def _inproj_kernel(x_ref, nw_ref, wa_ref, wg_ref, wup_ref, bg_ref, tri_ref, cos_ref, sin_ref,
                   f_ref, b_ref, v_ref, og_ref, decf_ref, decb_ref, dq_ref, dk_ref, dv_ref):
    for j in range(x_ref.shape[0] // INPROJ_ROWS):
        rows = pl.ds(j * INPROJ_ROWS, INPROJ_ROWS)
        blk = pl.ds(j, 1)
        _inproj_rows(x_ref.at[rows], nw_ref, wa_ref, wg_ref, wup_ref, bg_ref, tri_ref,
                     cos_ref.at[rows], sin_ref.at[rows], f_ref.at[rows], b_ref.at[rows],
                     v_ref.at[rows], og_ref.at[rows], decf_ref.at[blk], decb_ref.at[blk],
                     dq_ref.at[rows], dk_ref.at[rows], dv_ref.at[rows])


def _inproj_rows(x_ref, nw_ref, wa_ref, wg_ref, wup_ref, bg_ref, tri_ref, cos_ref, sin_ref,
                 f_ref, b_ref, v_ref, og_ref, decf_ref, decb_ref, dq_ref, dk_ref, dv_ref):
    tm = x_ref.shape[0]
    h = _rms(x_ref[...], nw_ref[...]).astype(BF16)

    def proj(off, width):
        return _dot(h, wa_ref[:, off:off + width])

    gdown = _dot(h, wg_ref[...]).astype(BF16)
    pre = _dot(gdown, wup_ref[...]) + bg_ref[...]

    off = 2 * GLA_QK
    v_ref[...] = proj(off, GLA_V).astype(BF16)
    off += GLA_V
    og_ref[...] = proj(off, GLA_V).astype(BF16)
    off += GLA_V

    cos = cos_ref[...]
    sin = sin_ref[...]
    lane = lax.broadcasted_iota(jnp.int32, (tm, LANES), 1)
    second_half = (lane & (DIFF_DH - 1)) >= (ROT_DIM // 2)

    def rope(t):
        slabs = []
        for s in range(t.shape[1] // LANES):
            ts = t[:, s * LANES:(s + 1) * LANES]
            partner = jnp.where(second_half, pltpu.roll(ts, ROT_DIM // 2, 1),
                                pltpu.roll(ts, LANES - ROT_DIM // 2, 1))
            slabs.append(ts * cos + partner * sin)
        return jnp.concatenate(slabs, axis=1)

    dq = proj(off, DIFF_QK)
    dq_ref[...] = (rope(dq) * (DIFF_DH ** -0.5 * LOG2E)).astype(BF16)
    off += DIFF_QK
    dk = proj(off, DIFF_QK)
    dk_ref[...] = rope(dk).astype(BF16)
    off += DIFF_QK
    dv_ref[...] = proj(off, DIFF_V).astype(BF16)

    g = (jnp.minimum(pre, 0.0) - jnp.log1p(jnp.exp(-jnp.abs(pre)))) * (1.0 / GLA_GATE_NORMALIZER)
    g_hi = g.astype(BF16)
    g_lo = (g - g_hi.astype(F32)).astype(BF16)
    g_split = jnp.concatenate([g_hi, g_lo], axis=1)
    tri = tri_ref[...]
    half = tri.shape[0]
    pre_sum = jnp.concatenate([_dot(tri, g_split[r:r + half]) for r in range(0, tm, half)], axis=0)
    prefix = pre_sum[:, :2 * GLA_QK] + pre_sum[:, 2 * GLA_QK:]
    totals = [prefix[c + GLA_CHUNK - 1:c + GLA_CHUNK] for c in range(0, tm, GLA_CHUNK)]
    total = jnp.concatenate([jnp.broadcast_to(t, (GLA_CHUNK, 2 * GLA_QK)) for t in totals], axis=0)
    dec = jnp.exp(jnp.concatenate(totals, axis=0))
    pf, pb = prefix[:, :GLA_QK], prefix[:, GLA_QK:]
    tf, tb = total[:, :GLA_QK], total[:, GLA_QK:]
    gb = g[:, GLA_QK:]

    qk = proj(0, 2 * GLA_QK)
    q = qk[:, :GLA_QK] * (GLA_DK ** -0.5)
    k = qk[:, GLA_QK:]
    for cum, rest, o_ref in ((pf, tf - pf, f_ref), (tb - pb + gb, pb - gb, b_ref)):
        o_ref[:, 0:GLA_QK] = (q * jnp.exp(cum)).astype(BF16)
        o_ref[:, GLA_QK:2 * GLA_QK] = (k * jnp.exp(-cum)).astype(BF16)
        o_ref[:, 2 * GLA_QK:3 * GLA_QK] = (k * jnp.exp(rest)).astype(BF16)
    decf_ref[0] = dec[:, :GLA_QK].T
    decb_ref[0] = dec[:, GLA_QK:].T


def _chunk_prefix_matrix(rows):
    r = np.arange(rows)[:, None]
    c = np.arange(rows)[None, :]
    return jnp.asarray(((r // GLA_CHUNK) == (c // GLA_CHUNK)) & (c <= r), BF16)


def _rope_tables(seq):
    inv = ROPE_THETA ** (-jnp.arange(0, ROT_DIM, 2, dtype=F32) / ROT_DIM)
    ang = jnp.arange(seq).astype(F32)[:, None] * inv[None, :]
    cos, sin = jnp.cos(ang), jnp.sin(ang)
    rest = DIFF_DH - ROT_DIM
    cos64 = jnp.concatenate([cos, cos, jnp.ones((seq, rest), F32)], axis=1)
    sin64 = jnp.concatenate([-sin, sin, jnp.zeros((seq, rest), F32)], axis=1)
    reps = LANES // DIFF_DH
    return jnp.tile(cos64, (1, reps)), jnp.tile(sin64, (1, reps))


def _in_proj(x2d, seq, w):
    tokens = x2d.shape[0]
    tm = INPROJ_TILE
    nchunk = INPROJ_ROWS // GLA_CHUNK
    tri = _chunk_prefix_matrix(PREFIX_ROWS)
    cos, sin = _rope_tables(seq)
    pos_blocks = seq // tm
    row = lambda i: (i, 0)
    pos = lambda i: (i % pos_blocks, 0)

    def tok(width, dtype=BF16):
        return jax.ShapeDtypeStruct((tokens, width), dtype)

    dec = jax.ShapeDtypeStruct((tokens // INPROJ_ROWS, GLA_QK, nchunk), F32)
    dec_spec = pl.BlockSpec((tm // INPROJ_ROWS, GLA_QK, nchunk), lambda i: (i, 0, 0))
    return pl.pallas_call(
        _inproj_kernel,
        grid=(tokens // tm,),
        in_specs=[pl.BlockSpec((tm, D_MODEL), row),
                  _const_spec((1, D_MODEL)),
                  _const_spec(w["w_a"].shape),
                  _const_spec(w["w_g"].shape),
                  _const_spec(w["w_up"].shape),
                  _const_spec(w["b_g"].shape),
                  _const_spec(tri.shape),
                  pl.BlockSpec((tm, LANES), pos),
                  pl.BlockSpec((tm, LANES), pos)],
        out_specs=[pl.BlockSpec((tm, 3 * GLA_QK), row),
                   pl.BlockSpec((tm, 3 * GLA_QK), row),
                   pl.BlockSpec((tm, GLA_V), row),
                   pl.BlockSpec((tm, GLA_V), row),
                   dec_spec,
                   dec_spec,
                   pl.BlockSpec((tm, DIFF_QK), row),
                   pl.BlockSpec((tm, DIFF_QK), row),
                   pl.BlockSpec((tm, DIFF_V), row)],
        out_shape=[tok(3 * GLA_QK), tok(3 * GLA_QK), tok(GLA_V), tok(GLA_V), dec, dec,
                   tok(DIFF_QK), tok(DIFF_QK), tok(DIFF_V)],
        compiler_params=_params(("parallel",)),
        name="in_proj",
    )(x2d, w["norm_mix_pre"], w["w_a"], w["w_g"], w["w_up"], w["b_g"], tri, cos, sin)


def _gla_kernel(f_ref, b_ref, vf_ref, vb_ref, decf_ref, decb_ref, of_ref, ob_ref, sf_ref, sb_ref):
    nchunk = f_ref.shape[0] // GLA_CHUNK

    @pl.when(pl.program_id(1) == 0)
    def _():
        sf_ref[...] = jnp.zeros_like(sf_ref)
        sb_ref[...] = jnp.zeros_like(sb_ref)

    qk_head = lax.broadcasted_iota(jnp.int32, (GLA_CHUNK, GLA_QK), 1) // GLA_DK
    v_head = lax.broadcasted_iota(jnp.int32, (GLA_CHUNK, GLA_V), 1) // GLA_DV
    s_head = lax.broadcasted_iota(jnp.int32, (GLA_QK, GLA_DV), 0) // GLA_DK
    a_row = lax.broadcasted_iota(jnp.int32, (GLA_CHUNK, GLA_QK), 0)
    a_col = lax.broadcasted_iota(jnp.int32, (GLA_CHUNK, GLA_QK), 1) % GLA_CHUNK

    def chunk_step(x_ref, v_ref, dec_ref, o_ref, s_ref, c, keep):
        rows = slice(c * GLA_CHUNK, (c + 1) * GLA_CHUNK)
        qe = x_ref[rows, 0:GLA_QK]
        ke = x_ref[rows, GLA_QK:2 * GLA_QK]
        kd = x_ref[rows, 2 * GLA_QK:3 * GLA_QK]
        v = v_ref[rows, :]
        per_group = INPROJ_ROWS // GLA_CHUNK
        dec = dec_ref[c // per_group, :, c % per_group:c % per_group + 1]
        ke_heads = jnp.concatenate([jnp.where(qk_head == h, ke, jnp.zeros_like(ke))
                                    for h in range(GLA_HEADS)], axis=0)
        a = jnp.where(keep, _dot_nt(qe, ke_heads), 0.0).astype(BF16)
        v_heads = jnp.concatenate([jnp.where(v_head == h, v, jnp.zeros_like(v))
                                   for h in range(GLA_HEADS)], axis=0)
        state = s_ref[...]
        state16 = state.astype(BF16)
        s_heads = jnp.concatenate([jnp.where(s_head == h, state16, jnp.zeros_like(state16))
                                   for h in range(GLA_HEADS)], axis=1)
        o_ref[rows, :] = (_dot(a, v_heads) + _dot(qe, s_heads)).astype(BF16)
        upd = jnp.concatenate([_dot_tn(kd[:, h * GLA_DK:(h + 1) * GLA_DK],
                                       v[:, h * GLA_DV:(h + 1) * GLA_DV])
                               for h in range(GLA_HEADS)], axis=0)
        s_ref[...] = dec * state + upd

    for c in range(nchunk):
        chunk_step(f_ref, vf_ref, decf_ref, of_ref, sf_ref, c, a_col <= a_row)
        chunk_step(b_ref, vb_ref, decb_ref, ob_ref, sb_ref, nchunk - 1 - c, a_col >= a_row)


def _gla(f, b, v, decf, decb, batch, seq):
    tokens = f.shape[0]
    lb = GLA_BLOCK
    nblk = seq // lb
    nchunk = lb // GLA_CHUNK
    fwd = lambda bi, i: (bi * nblk + i, 0)
    bwd = lambda bi, i: (bi * nblk + nblk - 1 - i, 0)
    fwd3 = lambda bi, i: (bi * nblk + i, 0, 0)
    bwd3 = lambda bi, i: (bi * nblk + nblk - 1 - i, 0, 0)
    out = jax.ShapeDtypeStruct((tokens, GLA_V), BF16)
    return pl.pallas_call(
        _gla_kernel,
        grid=(batch, nblk),
        in_specs=[pl.BlockSpec((lb, 3 * GLA_QK), fwd),
                  pl.BlockSpec((lb, 3 * GLA_QK), bwd),
                  pl.BlockSpec((lb, GLA_V), fwd),
                  pl.BlockSpec((lb, GLA_V), bwd),
                  pl.BlockSpec((lb // INPROJ_ROWS, GLA_QK, INPROJ_ROWS // GLA_CHUNK), fwd3),
                  pl.BlockSpec((lb // INPROJ_ROWS, GLA_QK, INPROJ_ROWS // GLA_CHUNK), bwd3)],
        out_specs=[pl.BlockSpec((lb, GLA_V), fwd),
                   pl.BlockSpec((lb, GLA_V), bwd)],
        out_shape=[out, out],
        scratch_shapes=[pltpu.VMEM((GLA_QK, GLA_DV), F32),
                        pltpu.VMEM((GLA_QK, GLA_DV), F32)],
        compiler_params=_params(("parallel", "arbitrary")),
        name="gla",
    )(f, b, v, v, decf, decb)


def _diff_kernel(q_ref, k_ref, v_ref, lq1_ref, lk1_ref, lq2_ref, lk2_ref, w_ref, o_ref,
                 qm_ref, m_ref, l_ref, acc_ref, s_ref, *, lam_init):
    ki = pl.program_id(2)
    tq = q_ref.shape[0]
    tk = k_ref.shape[0]
    width = 2 * DIFF_DH

    @pl.when(ki == 0)
    def _():
        m_ref[...] = jnp.full_like(m_ref, -jnp.inf)
        l_ref[...] = jnp.zeros_like(l_ref)
        acc_ref[...] = jnp.zeros_like(acc_ref)
        first = lax.broadcasted_iota(jnp.int32, (tq, width), 1) < DIFF_DH
        for h in range(DIFF_HEADS):
            qh = q_ref[:, h * width:(h + 1) * width]
            zero = jnp.zeros_like(qh)
            qm_ref[2 * h] = jnp.where(first, qh, zero)
            qm_ref[2 * h + 1] = jnp.where(first, zero, qh)

    ones = jnp.ones((ATTN_KEY_BLOCK, width), BF16)
    nstream = 2 * DIFF_HEADS

    def scores(s_idx):
        h = s_idx // 2
        s = _dot_nt(qm_ref[s_idx], k_ref[:, h * width:(h + 1) * width])
        s_ref[s_idx % 2] = s
        m_prev = m_ref[s_idx]
        m_next = jnp.maximum(m_prev, jnp.max(s, axis=1, keepdims=True))
        m_ref[s_idx] = m_next
        return m_prev, m_next

    maxes = scores(0)
    for s_idx in range(nstream):
        m_prev, m_next = maxes
        if s_idx + 1 < nstream:
            maxes = scores(s_idx + 1)
        h = s_idx // 2
        vh = v_ref[:, h * width:(h + 1) * width]
        m_wide = jnp.tile(m_next, (1, ATTN_KEY_BLOCK // LANES))
        pv = jnp.zeros((tq, 2 * width), F32)
        for k0 in range(0, tk, ATTN_KEY_BLOCK):
            keys = slice(k0, k0 + ATTN_KEY_BLOCK)
            pb = jnp.exp2(s_ref[s_idx % 2, :, keys] - m_wide).astype(BF16)
            pv = pv + _dot(pb, jnp.concatenate([vh[keys], ones], axis=1))
        alpha = jnp.exp2(m_prev - m_next)
        acc_ref[s_idx] = alpha * acc_ref[s_idx] + pv[:, :width]
        l_ref[s_idx] = alpha * l_ref[s_idx] + pv[:, width:]

    @pl.when(ki == pl.num_programs(2) - 1)
    def _():
        lam = (jnp.exp(jnp.sum(lq1_ref[...] * lk1_ref[...], axis=1, keepdims=True))
               - jnp.exp(jnp.sum(lq2_ref[...] * lk2_ref[...], axis=1, keepdims=True)) + lam_init)
        for h in range(DIFF_HEADS):
            o1 = acc_ref[2 * h] / l_ref[2 * h]
            o2 = acc_ref[2 * h + 1] / l_ref[2 * h + 1]
            o = _rms(o1 - lam * o2, w_ref[...]) * (1.0 - lam_init)
            o_ref[:, h * width:(h + 1) * width] = o.astype(BF16)


def _diff_attn(dq, dk, dv, w, batch, seq, lam_init):
    tokens = dq.shape[0]
    tq = min(ATTN_Q_TILE, seq)
    tk = min(ATTN_K_TILE, seq)
    nq, nk = seq // tq, seq // tk
    qmap = lambda b, qi, ki: (b * nq + qi, 0)
    kmap = lambda b, qi, ki: (b * nk + ki, 0)
    nstream = 2 * DIFF_HEADS
    vec = _const_spec((1, DIFF_DH))
    return pl.pallas_call(
        functools.partial(_diff_kernel, lam_init=lam_init),
        grid=(batch, nq, nk),
        in_specs=[pl.BlockSpec((tq, DIFF_QK), qmap),
                  pl.BlockSpec((tk, DIFF_QK), kmap),
                  pl.BlockSpec((tk, DIFF_V), kmap),
                  vec, vec, vec, vec,
                  _const_spec((1, 2 * DIFF_DH))],
        out_specs=pl.BlockSpec((tq, DIFF_V), qmap),
        out_shape=jax.ShapeDtypeStruct((tokens, DIFF_V), BF16),
        scratch_shapes=[pltpu.VMEM((nstream, tq, 2 * DIFF_DH), BF16),
                        pltpu.VMEM((nstream, tq, LANES), F32),
                        pltpu.VMEM((nstream, tq, LANES), F32),
                        pltpu.VMEM((nstream, tq, 2 * DIFF_DH), F32),
                        pltpu.VMEM((2, tq, tk), F32)],
        compiler_params=_params(("parallel", "parallel", "arbitrary")),
        name="diff_attn",
    )(dq, dk, dv, w["lambda_q1"], w["lambda_k1"], w["lambda_q2"], w["lambda_k2"], w["diff_subln_w"])


def _mix_xattn_kernel(x_ref, of_ref, ob_ref, og_ref, od_ref, k_ref, v_ref,
                      gnw_ref, wout_ref, npost_ref, nxpre_ref, wxq_ref, wxo_ref, nxpost_ref, y_ref):
    tm = x_ref.shape[0]
    parts = [slice(r, r + tm // MIX_ROW_SPLIT) for r in range(0, tm, tm // MIX_ROW_SPLIT)]

    def gla_out(rows):
        o = of_ref[rows, :].astype(F32) + ob_ref[rows, :].astype(F32)
        og = og_ref[rows, :].astype(F32)
        gate = og * (1.0 / (1.0 + jnp.exp(-og)))
        heads = []
        for h in range(GLA_HEADS):
            cols = slice(h * GLA_DV, (h + 1) * GLA_DV)
            heads.append((_rms(o[:, cols], gnw_ref[...]) * gate[:, cols]).astype(BF16))
        return jnp.concatenate(heads + [od_ref[rows, :]], axis=1)

    def attend(q):
        outs = []
        for h in range(XATTN_HEADS):
            cols = slice(h * XATTN_DH, (h + 1) * XATTN_DH)
            s = _dot_nt(q[:, cols], k_ref[:, cols])
            p = jnp.exp(s - jnp.max(s, axis=1, keepdims=True))
            denom = jnp.sum(p, axis=1, keepdims=True)
            outs.append((_dot(p.astype(BF16), v_ref[:, cols]) / denom).astype(BF16))
        return jnp.concatenate(outs, axis=1)

    mix_in = [gla_out(rows) for rows in parts]
    mix = [_dot(t, wout_ref[...]) for t in mix_in]
    x1 = [x_ref[rows, :] + _rms(t, npost_ref[...]) for rows, t in zip(parts, mix)]
    h2 = [_rms(t, nxpre_ref[...]).astype(BF16) for t in x1]
    q = [(_dot(t, wxq_ref[...]) * (XATTN_DH ** -0.5)).astype(BF16) for t in h2]
    att = [attend(t) for t in q]
    xo = [_dot(t, wxo_ref[...]) for t in att]
    for rows, res, t in zip(parts, x1, xo):
        y_ref[rows, :] = res + _rms(t, nxpost_ref[...])


def _mix_xattn(x2d, o_f, o_b, og, o_diff, mem_k, mem_v, seq, w):
    tokens = x2d.shape[0]
    tm = MIX_ROW_SPLIT * TOKEN_TILE
    per_seq = seq // tm
    row = lambda i: (i, 0)
    memmap = lambda i: (i // per_seq, 0)
    half = pl.BlockSpec((tm, GLA_V), row)
    sq = _const_spec((D_MODEL, D_MODEL))
    nv = _const_spec((1, D_MODEL))
    return pl.pallas_call(
        _mix_xattn_kernel,
        grid=(tokens // tm,),
        in_specs=[pl.BlockSpec((tm, D_MODEL), row), half, half, half, half,
                  pl.BlockSpec((MEM_LEN, D_MODEL), memmap),
                  pl.BlockSpec((MEM_LEN, D_MODEL), memmap),
                  _const_spec((1, GLA_DV)), sq, nv, nv, sq, sq, nv],
        out_specs=pl.BlockSpec((tm, D_MODEL), row),
        out_shape=jax.ShapeDtypeStruct((tokens, D_MODEL), F32),
        compiler_params=_params(("parallel",)),
        name="mix_xattn",
    )(x2d, o_f, o_b, og, o_diff, mem_k, mem_v, w["gla_norm_w"], w["w_out"], w["norm_mix_post"],
      w["norm_xattn_pre"], w["w_xq"], w["w_xo"], w["norm_xattn_post"])


def _ffn_kernel(x_ref, npre_ref, wg_ref, wu_ref, wd_ref, npost_ref, y_ref):
    tm = x_ref.shape[0]
    parts = [slice(r, r + tm // MIX_ROW_SPLIT) for r in range(0, tm, tm // MIX_ROW_SPLIT)]
    h = [_rms(x_ref[rows, :], npre_ref[...]).astype(BF16) for rows in parts]
    g = [_dot(t, wg_ref[...]) for t in h]
    u = [_dot(t, wu_ref[...]) for t in h]
    a = [(tg * (1.0 / (1.0 + jnp.exp(-tg))) * tu).astype(BF16) for tg, tu in zip(g, u)]
    d = [_dot(t, wd_ref[...]) for t in a]
    for rows, t in zip(parts, d):
        y_ref[rows, :] = x_ref[rows, :] + _rms(t, npost_ref[...])


def _ffn(x2d, w):
    tokens = x2d.shape[0]
    tm = MIX_ROW_SPLIT * TOKEN_TILE
    row = lambda i: (i, 0)
    nv = _const_spec((1, D_MODEL))
    return pl.pallas_call(
        _ffn_kernel,
        grid=(tokens // tm,),
        in_specs=[pl.BlockSpec((tm, D_MODEL), row), nv,
                  _const_spec((D_MODEL, D_FF)), _const_spec((D_MODEL, D_FF)),
                  _const_spec((D_FF, D_MODEL)), nv],
        out_specs=pl.BlockSpec((tm, D_MODEL), row),
        out_shape=jax.ShapeDtypeStruct((tokens, D_MODEL), F32),
        compiler_params=_params(("parallel",)),
        name="ffn",
    )(x2d, w["norm_ffn_pre"], w["w_ffn_gate"], w["w_ffn_up"], w["w_ffn_down"], w["norm_ffn_post"])


def _prepare_weights(p):
    w_in = p["w_in"]
    gates_at = 2 * GLA_QK + GLA_V
    gates_end = gates_at + 2 * GLA_GATE_RANK
    w = {}
    w["w_a"] = jnp.concatenate([w_in[:, :gates_at], w_in[:, gates_end:]], axis=1).astype(BF16)
    w["w_g"] = jnp.pad(w_in[:, gates_at:gates_end], ((0, 0), (0, LANES - 2 * GLA_GATE_RANK))).astype(BF16)
    w_up = jnp.zeros((LANES, 2 * GLA_QK), F32)
    w_up = w_up.at[:GLA_GATE_RANK, :GLA_QK].set(p["w_gate_up_f"])
    w_up = w_up.at[GLA_GATE_RANK:2 * GLA_GATE_RANK, GLA_QK:].set(p["w_gate_up_b"])
    w["w_up"] = w_up.astype(BF16)
    w["b_g"] = jnp.concatenate([p["b_gate_f"], p["b_gate_b"]])[None, :]
    for name in ("w_out", "w_xq", "w_xkv", "w_xo", "w_ffn_gate", "w_ffn_up", "w_ffn_down"):
        w[name] = p[name].astype(BF16)
    for name in ("norm_mix_pre", "gla_norm_w", "lambda_q1", "lambda_k1", "lambda_q2", "lambda_k2",
                 "diff_subln_w", "norm_mix_post", "norm_xattn_pre", "norm_mem", "norm_xattn_post",
                 "norm_ffn_pre", "norm_ffn_post"):
        w[name] = p[name][None, :]
    return w


def _encoder_layer(x, mem, w, lam_init):
    batch, seq, _ = x.shape
    assert seq % (MIX_ROW_SPLIT * TOKEN_TILE) == 0 and seq % INPROJ_TILE == 0 and mem.shape[1] == MEM_LEN
    x2d = x.reshape(batch * seq, D_MODEL)
    mem_k, mem_v = _mem_kv(mem.reshape(batch * MEM_LEN, D_MODEL), w["norm_mem"], w["w_xkv"])
    f, b, v, og, decf, decb, dq, dk, dv = _in_proj(x2d, seq, w)
    o_f, o_b = _gla(f, b, v, decf, decb, batch, seq)
    o_diff = _diff_attn(dq, dk, dv, w, batch, seq, lam_init)
    x2 = _mix_xattn(x2d, o_f, o_b, og, o_diff, mem_k, mem_v, seq, w)
    return _ffn(x2, w).reshape(batch, seq, D_MODEL)


def kernel(x_prompt, x_sample, mem_prompt, mem_sample, norm_mix_pre, w_in, w_gate_up_f, b_gate_f,
           w_gate_up_b, b_gate_b, gla_norm_w, lambda_q1, lambda_k1, lambda_q2, lambda_k2, diff_subln_w,
           w_out, norm_mix_post, norm_xattn_pre, norm_mem, w_xq, w_xkv, w_xo, norm_xattn_post,
           norm_ffn_pre, w_ffn_gate, w_ffn_up, w_ffn_down, norm_ffn_post):
    stacked = dict(
        norm_mix_pre=norm_mix_pre, w_in=w_in, w_gate_up_f=w_gate_up_f, b_gate_f=b_gate_f,
        w_gate_up_b=w_gate_up_b, b_gate_b=b_gate_b, gla_norm_w=gla_norm_w, lambda_q1=lambda_q1,
        lambda_k1=lambda_k1, lambda_q2=lambda_q2, lambda_k2=lambda_k2, diff_subln_w=diff_subln_w,
        w_out=w_out, norm_mix_post=norm_mix_post, norm_xattn_pre=norm_xattn_pre, norm_mem=norm_mem,
        w_xq=w_xq, w_xkv=w_xkv, w_xo=w_xo, norm_xattn_post=norm_xattn_post,
        norm_ffn_pre=norm_ffn_pre, w_ffn_gate=w_ffn_gate, w_ffn_up=w_ffn_up, w_ffn_down=w_ffn_down,
        norm_ffn_post=norm_ffn_post)
    y_prompt, y_sample = x_prompt, x_sample
    for layer in range(w_in.shape[0]):
        lam_init = 0.8 - 0.6 * math.exp(-0.3 * layer)
        w = _prepare_weights({name: t[layer] for name, t in stacked.items()})
        y_prompt = _encoder_layer(y_prompt, mem_prompt, w, lam_init)
        y_sample = _encoder_layer(y_sample, mem_sample, w, lam_init)
    return (y_prompt, y_sample)
```

```python
import functools
import math

import numpy as np
import jax
import jax.numpy as jnp
from jax import lax
from jax.experimental import pallas as pl
from jax.experimental.pallas import tpu as pltpu

F32 = jnp.float32
BF16 = jnp.bfloat16

D_MODEL = 1024
GLA_HEADS = 4
GLA_DK = 64
GLA_DV = 128
GLA_QK = GLA_HEADS * GLA_DK
GLA_V = GLA_HEADS * GLA_DV
GLA_GATE_RANK = 16
GLA_GATE_NORMALIZER = 16.0
GLA_CHUNK = 64
DIFF_HEADS = 4
DIFF_DH = 64
DIFF_QK = DIFF_HEADS * 2 * DIFF_DH
DIFF_V = DIFF_HEADS * 2 * DIFF_DH
ROT_DIM = DIFF_DH // 4
ROPE_THETA = 500000.0
MEM_LEN = 256
XATTN_HEADS = 4
XATTN_DH = D_MODEL // XATTN_HEADS
D_FF = 2816
RMS_EPS = 1e-6

LANES = 128
TOKEN_TILE = 256
GLA_BLOCK = 1024
INPROJ_TILE = 1024
INPROJ_ROWS = 512
PREFIX_ROWS = 256
ATTN_Q_TILE = 512
ATTN_K_TILE = 2048
MIX_ROW_SPLIT = 4
ATTN_KEY_BLOCK = 256
LOG2E = math.log2(math.e)
VMEM_LIMIT = 56 * 1024 * 1024

NT_DIMS = (((1,), (1,)), ((), ()))
TN_DIMS = (((0,), (0,)), ((), ()))


def _rms(x, w):
    return x * lax.rsqrt(jnp.mean(x * x, axis=-1, keepdims=True) + RMS_EPS) * w


def _silu(x):
    half = 0.5 * x
    return half + half * jnp.tanh(half)


def _dot(a, b):
    return jnp.dot(a, b, preferred_element_type=F32)


def _dot_nt(a, b):
    return lax.dot_general(a, b, NT_DIMS, preferred_element_type=F32)


def _dot_tn(a, b):
    return lax.dot_general(a, b, TN_DIMS, preferred_element_type=F32)


def _const_spec(shape):
    return pl.BlockSpec(shape, lambda *_: (0,) * len(shape), pipeline_mode=pl.Buffered(1))


def _params(semantics):
    return pltpu.CompilerParams(dimension_semantics=semantics, vmem_limit_bytes=VMEM_LIMIT)


def _mem_kv_kernel(m_ref, nw_ref, w_ref, k_ref, v_ref):
    m = _rms(m_ref[...], nw_ref[...]).astype(BF16)
    kv = _dot(m, w_ref[...])
    k_ref[...] = kv[:, :D_MODEL].astype(BF16)
    v_ref[...] = kv[:, D_MODEL:].astype(BF16)


def _mem_kv(mem2d, norm_w, w_xkv):
    rows = mem2d.shape[0]
    out = jax.ShapeDtypeStruct((rows, D_MODEL), BF16)
    return pl.pallas_call(
        _mem_kv_kernel,
        grid=(rows // MEM_LEN,),
        in_specs=[pl.BlockSpec((MEM_LEN, D_MODEL), lambda i: (i, 0)),
                  _const_spec((1, D_MODEL)),
                  _const_spec((D_MODEL, 2 * D_MODEL))],
        out_specs=[pl.BlockSpec((MEM_LEN, D_MODEL), lambda i: (i, 0))] * 2,
        out_shape=[out, out],
        compiler_params=_params(("parallel",)),
        name="mem_kv",
    )(mem2d, norm_w, w_xkv)


def _inproj_kernel(x_ref, nw_ref, wa_ref, wg_ref, wup_ref, bg_ref, tri_ref, cos_ref, sin_ref,
                   f_ref, b_ref, v_ref, og_ref, decf_ref, decb_ref, dq_ref, dk_ref, dv_ref):
    for j in range(x_ref.shape[0] // INPROJ_ROWS):
        rows = pl.ds(j * INPROJ_ROWS, INPROJ_ROWS)
        blk = pl.ds(j, 1)
        _inproj_rows(x_ref.at[rows], nw_ref, wa_ref, wg_ref, wup_ref, bg_ref, tri_ref,
                     cos_ref.at[rows], sin_ref.at[rows], f_ref.at[rows], b_ref.at[rows],
                     v_ref.at[rows], og_ref.at[rows], decf_ref.at[blk], decb_ref.at[blk],
                     dq_ref.at[rows], dk_ref.at[rows], dv_ref.at[rows])


def _inproj_rows(x_ref, nw_ref, wa_ref, wg_ref, wup_ref, bg_ref, tri_ref, cos_ref, sin_ref,
                 f_ref, b_ref, v_ref, og_ref, decf_ref, decb_ref, dq_ref, dk_ref, dv_ref):
    tm = x_ref.shape[0]
    h = _rms(x_ref[...], nw_ref[...]).astype(BF16)

    def proj(off, width):
        return _dot(h, wa_ref[:, off:off + width])

    gdown = _dot(h, wg_ref[...]).astype(BF16)
    pre = _dot(gdown, wup_ref[...]) + bg_ref[...]

    off = 2 * GLA_QK
    v_ref[...] = proj(off, GLA_V).astype(BF16)
    off += GLA_V
    og_ref[...] = proj(off, GLA_V).astype(BF16)
    off += GLA_V

    cos = cos_ref[...]
    sin = sin_ref[...]
    lane = lax.broadcasted_iota(jnp.int32, (tm, LANES), 1)
    second_half = (lane & (DIFF_DH - 1)) >= (ROT_DIM // 2)

    def rope(t):
        slabs = []
        for s in range(t.shape[1] // LANES):
            ts = t[:, s * LANES:(s + 1) * LANES]
            partner = jnp.where(second_half, pltpu.roll(ts, ROT_DIM // 2, 1),
                                pltpu.roll(ts, LANES - ROT_DIM // 2, 1))
            slabs.append(ts * cos + partner * sin)
        return jnp.concatenate(slabs, axis=1)

    dq = proj(off, DIFF_QK)
    dq_ref[...] = (rope(dq) * (DIFF_DH ** -0.5 * LOG2E)).astype(BF16)
    off += DIFF_QK
    dk = proj(off, DIFF_QK)
    dk_ref[...] = rope(dk).astype(BF16)
    off += DIFF_QK
    dv_ref[...] = proj(off, DIFF_V).astype(BF16)

    g = (jnp.minimum(pre, 0.0) - jnp.log1p(jnp.exp(-jnp.abs(pre)))) * (1.0 / GLA_GATE_NORMALIZER)
    g_hi = g.astype(BF16)
    g_lo = (g - g_hi.astype(F32)).astype(BF16)
    g_split = jnp.concatenate([g_hi, g_lo], axis=1)
    tri = tri_ref[...]
    half = tri.shape[0]
    pre_sum = jnp.concatenate([_dot(tri, g_split[r:r + half]) for r in range(0, tm, half)], axis=0)
    prefix = pre_sum[:, :2 * GLA_QK] + pre_sum[:, 2 * GLA_QK:]
    totals = [prefix[c + GLA_CHUNK - 1:c + GLA_CHUNK] for c in range(0, tm, GLA_CHUNK)]
    total = jnp.concatenate([jnp.broadcast_to(t, (GLA_CHUNK, 2 * GLA_QK)) for t in totals], axis=0)
    dec = jnp.exp(jnp.concatenate(totals, axis=0))
    pf, pb = prefix[:, :GLA_QK], prefix[:, GLA_QK:]
    tf, tb = total[:, :GLA_QK], total[:, GLA_QK:]
    gb = g[:, GLA_QK:]

    qk = proj(0, 2 * GLA_QK)
    q = qk[:, :GLA_QK] * (GLA_DK ** -0.5)
    k = qk[:, GLA_QK:]
    for cum, rest, o_ref in ((pf, tf - pf, f_ref), (tb - pb + gb, pb - gb, b_ref)):
        o_ref[:, 0:GLA_QK] = (q * jnp.exp(cum)).astype(BF16)
        o_ref[:, GLA_QK:2 * GLA_QK] = (k * jnp.exp(-cum)).astype(BF16)
        o_ref[:, 2 * GLA_QK:3 * GLA_QK] = (k * jnp.exp(rest)).astype(BF16)
    decf_ref[0] = dec[:, :GLA_QK].T
    decb_ref[0] = dec[:, GLA_QK:].T


def _chunk_prefix_matrix(rows):
    r = np.arange(rows)[:, None]
    c = np.arange(rows)[None, :]
    return jnp.asarray(((r // GLA_CHUNK) == (c // GLA_CHUNK)) & (c <= r), BF16)


def _rope_tables(seq):
    inv = ROPE_THETA ** (-jnp.arange(0, ROT_DIM, 2, dtype=F32) / ROT_DIM)
    ang = jnp.arange(seq).astype(F32)[:, None] * inv[None, :]
    cos, sin = jnp.cos(ang), jnp.sin(ang)
    rest = DIFF_DH - ROT_DIM
    cos64 = jnp.concatenate([cos, cos, jnp.ones((seq, rest), F32)], axis=1)
    sin64 = jnp.concatenate([-sin, sin, jnp.zeros((seq, rest), F32)], axis=1)
    reps = LANES // DIFF_DH
    return jnp.tile(cos64, (1, reps)), jnp.tile(sin64, (1, reps))


def _in_proj(x2d, seq, w):
    tokens = x2d.shape[0]
    tm = INPROJ_TILE
    nchunk = INPROJ_ROWS // GLA_CHUNK
    tri = _chunk_prefix_matrix(PREFIX_ROWS)
    cos, sin = _rope_tables(seq)
    pos_blocks = seq // tm
    row = lambda i: (i, 0)
    pos = lambda i: (i % pos_blocks, 0)

    def tok(width, dtype=BF16):
        return jax.ShapeDtypeStruct((tokens, width), dtype)

    dec = jax.ShapeDtypeStruct((tokens // INPROJ_ROWS, GLA_QK, nchunk), F32)
    dec_spec = pl.BlockSpec((tm // INPROJ_ROWS, GLA_QK, nchunk), lambda i: (i, 0, 0))
    return pl.pallas_call(
        _inproj_kernel,
        grid=(tokens // tm,),
        in_specs=[pl.BlockSpec((tm, D_MODEL), row),
                  _const_spec((1, D_MODEL)),
                  _const_spec(w["w_a"].shape),
                  _const_spec(w["w_g"].shape),
                  _const_spec(w["w_up"].shape),
                  _const_spec(w["b_g"].shape),
                  _const_spec(tri.shape),
                  pl.BlockSpec((tm, LANES), pos),
                  pl.BlockSpec((tm, LANES), pos)],
        out_specs=[pl.BlockSpec((tm, 3 * GLA_QK), row),
                   pl.BlockSpec((tm, 3 * GLA_QK), row),
                   pl.BlockSpec((tm, GLA_V), row),
                   pl.BlockSpec((tm, GLA_V), row),
                   dec_spec,
                   dec_spec,
                   pl.BlockSpec((tm, DIFF_QK), row),
                   pl.BlockSpec((tm, DIFF_QK), row),
                   pl.BlockSpec((tm, DIFF_V), row)],
        out_shape=[tok(3 * GLA_QK), tok(3 * GLA_QK), tok(GLA_V), tok(GLA_V), dec, dec,
                   tok(DIFF_QK), tok(DIFF_QK), tok(DIFF_V)],
        compiler_params=_params(("parallel",)),
        name="in_proj",
    )(x2d, w["norm_mix_pre"], w["w_a"], w["w_g"], w["w_up"], w["b_g"], tri, cos, sin)


def _gla_kernel(f_ref, b_ref, vf_ref, vb_ref, decf_ref, decb_ref, of_ref, ob_ref, sf_ref, sb_ref):
    nchunk = f_ref.shape[0] // GLA_CHUNK

    @pl.when(pl.program_id(1) == 0)
    def _():
        sf_ref[...] = jnp.zeros_like(sf_ref)
        sb_ref[...] = jnp.zeros_like(sb_ref)

    qk_head = lax.broadcasted_iota(jnp.int32, (GLA_CHUNK, GLA_QK), 1) // GLA_DK
    v_head = lax.broadcasted_iota(jnp.int32, (GLA_CHUNK, GLA_V), 1) // GLA_DV
    s_head = lax.broadcasted_iota(jnp.int32, (GLA_QK, GLA_DV), 0) // GLA_DK
    a_row = lax.broadcasted_iota(jnp.int32, (GLA_CHUNK, GLA_QK), 0)
    a_col = lax.broadcasted_iota(jnp.int32, (GLA_CHUNK, GLA_QK), 1) % GLA_CHUNK

    def chunk_step(x_ref, v_ref, dec_ref, o_ref, s_ref, c, keep):
        rows = slice(c * GLA_CHUNK, (c + 1) * GLA_CHUNK)
        qe = x_ref[rows, 0:GLA_QK]
        ke = x_ref[rows, GLA_QK:2 * GLA_QK]
        kd = x_ref[rows, 2 * GLA_QK:3 * GLA_QK]
        v = v_ref[rows, :]
        per_group = INPROJ_ROWS // GLA_CHUNK
        dec = dec_ref[c // per_group, :, c % per_group:c % per_group + 1]
        ke_heads = jnp.concatenate([jnp.where(qk_head == h, ke, jnp.zeros_like(ke))
                                    for h in range(GLA_HEADS)], axis=0)
        a = jnp.where(keep, _dot_nt(qe, ke_heads), 0.0).astype(BF16)
        v_heads = jnp.concatenate([jnp.where(v_head == h, v, jnp.zeros_like(v))
                                   for h in range(GLA_HEADS)], axis=0)
        state = s_ref[...]
        state16 = state.astype(BF16)
        s_heads = jnp.concatenate([jnp.where(s_head == h, state16, jnp.zeros_like(state16))
                                   for h in range(GLA_HEADS)], axis=1)
        o_ref[rows, :] = (_dot(a, v_heads) + _dot(qe, s_heads)).astype(BF16)
        upd = jnp.concatenate([_dot_tn(kd[:, h * GLA_DK:(h + 1) * GLA_DK],
                                       v[:, h * GLA_DV:(h + 1) * GLA_DV])
                               for h in range(GLA_HEADS)], axis=0)
        s_ref[...] = dec * state + upd

    for c in range(nchunk):
        chunk_step(f_ref, vf_ref, decf_ref, of_ref, sf_ref, c, a_col <= a_row)
        chunk_step(b_ref, vb_ref, decb_ref, ob_ref, sb_ref, nchunk - 1 - c, a_col >= a_row)


def _gla(f, b, v, decf, decb, batch, seq):
    tokens = f.shape[0]
    lb = GLA_BLOCK
    nblk = seq // lb
    nchunk = lb // GLA_CHUNK
    fwd = lambda bi, i: (bi * nblk + i, 0)
    bwd = lambda bi, i: (bi * nblk + nblk - 1 - i, 0)
    fwd3 = lambda bi, i: (bi * nblk + i, 0, 0)
    bwd3 = lambda bi, i: (bi * nblk + nblk - 1 - i, 0, 0)
    out = jax.ShapeDtypeStruct((tokens, GLA_V), BF16)
    return pl.pallas_call(
        _gla_kernel,
        grid=(batch, nblk),
        in_specs=[pl.BlockSpec((lb, 3 * GLA_QK), fwd),
                  pl.BlockSpec((lb, 3 * GLA_QK), bwd),
                  pl.BlockSpec((lb, GLA_V), fwd),
                  pl.BlockSpec((lb, GLA_V), bwd),
                  pl.BlockSpec((lb // INPROJ_ROWS, GLA_QK, INPROJ_ROWS // GLA_CHUNK), fwd3),
                  pl.BlockSpec((lb // INPROJ_ROWS, GLA_QK, INPROJ_ROWS // GLA_CHUNK), bwd3)],
        out_specs=[pl.BlockSpec((lb, GLA_V), fwd),
                   pl.BlockSpec((lb, GLA_V), bwd)],
        out_shape=[out, out],
        scratch_shapes=[pltpu.VMEM((GLA_QK, GLA_DV), F32),
                        pltpu.VMEM((GLA_QK, GLA_DV), F32)],
        compiler_params=_params(("parallel", "arbitrary")),
        name="gla",
    )(f, b, v, v, decf, decb)


def _diff_kernel(q_ref, k_ref, v_ref, lq1_ref, lk1_ref, lq2_ref, lk2_ref, w_ref, o_ref,
                 qm_ref, m_ref, l_ref, acc_ref, s_ref, *, lam_init):
    ki = pl.program_id(2)
    tq = q_ref.shape[0]
    tk = k_ref.shape[0]
    width = 2 * DIFF_DH

    @pl.when(ki == 0)
    def _():
        m_ref[...] = jnp.full_like(m_ref, -jnp.inf)
        l_ref[...] = jnp.zeros_like(l_ref)
        acc_ref[...] = jnp.zeros_like(acc_ref)
        first = lax.broadcasted_iota(jnp.int32, (tq, width), 1) < DIFF_DH
        for h in range(DIFF_HEADS):
            qh = q_ref[:, h * width:(h + 1) * width]
            zero = jnp.zeros_like(qh)
            qm_ref[2 * h] = jnp.where(first, qh, zero)
            qm_ref[2 * h + 1] = jnp.where(first, zero, qh)

    ones = jnp.ones((ATTN_KEY_BLOCK, width), BF16)
    nstream = 2 * DIFF_HEADS

    def scores(s_idx):
        h = s_idx // 2
        s = _dot_nt(qm_ref[s_idx], k_ref[:, h * width:(h + 1) * width])
        s_ref[s_idx % 2] = s
        m_prev = m_ref[s_idx]
        m_next = jnp.maximum(m_prev, jnp.max(s, axis=1, keepdims=True))
        m_ref[s_idx] = m_next
        return m_prev, m_next

    maxes = scores(0)
    for s_idx in range(nstream):
        m_prev, m_next = maxes
        if s_idx + 1 < nstream:
            maxes = scores(s_idx + 1)
        h = s_idx // 2
        vh = v_ref[:, h * width:(h + 1) * width]
        m_wide = jnp.tile(m_next, (1, ATTN_KEY_BLOCK // LANES))
        pv = jnp.zeros((tq, 2 * width), F32)
        for k0 in range(0, tk, ATTN_KEY_BLOCK):
            keys = slice(k0, k0 + ATTN_KEY_BLOCK)
            pb = jnp.exp2(s_ref[s_idx % 2, :, keys] - m_wide).astype(BF16)
            pv = pv + _dot(pb, jnp.concatenate([vh[keys], ones], axis=1))
        alpha = jnp.exp2(m_prev - m_next)
        acc_ref[s_idx] = alpha * acc_ref[s_idx] + pv[:, :width]
        l_ref[s_idx] = alpha * l_ref[s_idx] + pv[:, width:]

    @pl.when(ki == pl.num_programs(2) - 1)
    def _():
        lam = (jnp.exp(jnp.sum(lq1_ref[...] * lk1_ref[...], axis=1, keepdims=True))
               - jnp.exp(jnp.sum(lq2_ref[...] * lk2_ref[...], axis=1, keepdims=True)) + lam_init)
        for h in range(DIFF_HEADS):
            o1 = acc_ref[2 * h] / l_ref[2 * h]
            o2 = acc_ref[2 * h + 1] / l_ref[2 * h + 1]
            o = _rms(o1 - lam * o2, w_ref[...]) * (1.0 - lam_init)
            o_ref[:, h * width:(h + 1) * width] = o.astype(BF16)


def _diff_attn(dq, dk, dv, w, batch, seq, lam_init):
    tokens = dq.shape[0]
    tq = min(ATTN_Q_TILE, seq)
    tk = min(ATTN_K_TILE, seq)
    nq, nk = seq // tq, seq // tk
    qmap = lambda b, qi, ki: (b * nq + qi, 0)
    kmap = lambda b, qi, ki: (b * nk + ki, 0)
    nstream = 2 * DIFF_HEADS
    vec = _const_spec((1, DIFF_DH))
    return pl.pallas_call(
        functools.partial(_diff_kernel, lam_init=lam_init),
        grid=(batch, nq, nk),
        in_specs=[pl.BlockSpec((tq, DIFF_QK), qmap),
                  pl.BlockSpec((tk, DIFF_QK), kmap),
                  pl.BlockSpec((tk, DIFF_V), kmap),
                  vec, vec, vec, vec,
                  _const_spec((1, 2 * DIFF_DH))],
        out_specs=pl.BlockSpec((tq, DIFF_V), qmap),
        out_shape=jax.ShapeDtypeStruct((tokens, DIFF_V), BF16),
        scratch_shapes=[pltpu.VMEM((nstream, tq, 2 * DIFF_DH), BF16),
                        pltpu.VMEM((nstream, tq, LANES), F32),
                        pltpu.VMEM((nstream, tq, LANES), F32),
                        pltpu.VMEM((nstream, tq, 2 * DIFF_DH), F32),
                        pltpu.VMEM((2, tq, tk), F32)],
        compiler_params=_params(("parallel", "parallel", "arbitrary")),
        name="diff_attn",
    )(dq, dk, dv, w["lambda_q1"], w["lambda_k1"], w["lambda_q2"], w["lambda_k2"], w["diff_subln_w"])


def _mix_xattn_kernel(x_ref, of_ref, ob_ref, og_ref, od_ref, k_ref, v_ref,
                      gnw_ref, wout_ref, npost_ref, nxpre_ref, wxq_ref, wxo_ref, nxpost_ref, y_ref):
    tm = x_ref.shape[0]
    parts = [slice(r, r + tm // MIX_ROW_SPLIT) for r in range(0, tm, tm // MIX_ROW_SPLIT)]

    def gla_out(rows):
        o = of_ref[rows, :].astype(F32) + ob_ref[rows, :].astype(F32)
        og = og_ref[rows, :].astype(F32)
        gate = _silu(og)
        heads = []
        for h in range(GLA_HEADS):
            cols = slice(h * GLA_DV, (h + 1) * GLA_DV)
            heads.append((_rms(o[:, cols], gnw_ref[...]) * gate[:, cols]).astype(BF16))
        return jnp.concatenate(heads + [od_ref[rows, :]], axis=1)

    def attend(q):
        outs = []
        for h in range(XATTN_HEADS):
            cols = slice(h * XATTN_DH, (h + 1) * XATTN_DH)
            s = _dot_nt(q[:, cols], k_ref[:, cols])
            p = jnp.exp(s - jnp.max(s, axis=1, keepdims=True))
            denom = jnp.sum(p, axis=1, keepdims=True)
            outs.append((_dot(p.astype(BF16), v_ref[:, cols]) / denom).astype(BF16))
        return jnp.concatenate(outs, axis=1)

    mix_in = [gla_out(rows) for rows in parts]
    mix = [_dot(t, wout_ref[...]) for t in mix_in]
    x1 = [x_ref[rows, :] + _rms(t, npost_ref[...]) for rows, t in zip(parts, mix)]
    h2 = [_rms(t, nxpre_ref[...]).astype(BF16) for t in x1]
    q = [(_dot(t, wxq_ref[...]) * (XATTN_DH ** -0.5)).astype(BF16) for t in h2]
    att = [attend(t) for t in q]
    xo = [_dot(t, wxo_ref[...]) for t in att]
    for rows, res, t in zip(parts, x1, xo):
        y_ref[rows, :] = res + _rms(t, nxpost_ref[...])


def _mix_xattn(x2d, o_f, o_b, og, o_diff, mem_k, mem_v, seq, w):
    tokens = x2d.shape[0]
    tm = MIX_ROW_SPLIT * TOKEN_TILE
    per_seq = seq // tm
    row = lambda i: (i, 0)
    memmap = lambda i: (i // per_seq, 0)
    half = pl.BlockSpec((tm, GLA_V), row)
    sq = _const_spec((D_MODEL, D_MODEL))
    nv = _const_spec((1, D_MODEL))
    return pl.pallas_call(
        _mix_xattn_kernel,
        grid=(tokens // tm,),
        in_specs=[pl.BlockSpec((tm, D_MODEL), row), half, half, half, half,
                  pl.BlockSpec((MEM_LEN, D_MODEL), memmap),
                  pl.BlockSpec((MEM_LEN, D_MODEL), memmap),
                  _const_spec((1, GLA_DV)), sq, nv, nv, sq, sq, nv],
        out_specs=pl.BlockSpec((tm, D_MODEL), row),
        out_shape=jax.ShapeDtypeStruct((tokens, D_MODEL), F32),
        compiler_params=_params(("parallel",)),
        name="mix_xattn",
    )(x2d, o_f, o_b, og, o_diff, mem_k, mem_v, w["gla_norm_w"], w["w_out"], w["norm_mix_post"],
      w["norm_xattn_pre"], w["w_xq"], w["w_xo"], w["norm_xattn_post"])


def _ffn_kernel(x_ref, npre_ref, wg_ref, wu_ref, wd_ref, npost_ref, y_ref):
    tm = x_ref.shape[0]
    parts = [slice(r, r + tm // MIX_ROW_SPLIT) for r in range(0, tm, tm // MIX_ROW_SPLIT)]
    h = [_rms(x_ref[rows, :], npre_ref[...]).astype(BF16) for rows in parts]
    g = [_dot(t, wg_ref[...]) for t in h]
    u = [_dot(t, wu_ref[...]) for t in h]
    a = [(_silu(tg) * tu).astype(BF16) for tg, tu in zip(g, u)]
    d = [_dot(t, wd_ref[...]) for t in a]
    for rows, t in zip(parts, d):
        y_ref[rows, :] = x_ref[rows, :] + _rms(t, npost_ref[...])


def _ffn(x2d, w):
    tokens = x2d.shape[0]
    tm = MIX_ROW_SPLIT * TOKEN_TILE
    row = lambda i: (i, 0)
    nv = _const_spec((1, D_MODEL))
    return pl.pallas_call(
        _ffn_kernel,
        grid=(tokens // tm,),
        in_specs=[pl.BlockSpec((tm, D_MODEL), row), nv,
                  _const_spec((D_MODEL, D_FF)), _const_spec((D_MODEL, D_FF)),
                  _const_spec((D_FF, D_MODEL)), nv],
        out_specs=pl.BlockSpec((tm, D_MODEL), row),
        out_shape=jax.ShapeDtypeStruct((tokens, D_MODEL), F32),
        compiler_params=_params(("parallel",)),
        name="ffn",
    )(x2d, w["norm_ffn_pre"], w["w_ffn_gate"], w["w_ffn_up"], w["w_ffn_down"], w["norm_ffn_post"])


def _prepare_weights(p):
    w_in = p["w_in"]
    gates_at = 2 * GLA_QK + GLA_V
    gates_end = gates_at + 2 * GLA_GATE_RANK
    w = {}
    w["w_a"] = jnp.concatenate([w_in[:, :gates_at], w_in[:, gates_end:]], axis=1).astype(BF16)
    w["w_g"] = jnp.pad(w_in[:, gates_at:gates_end], ((0, 0), (0, LANES - 2 * GLA_GATE_RANK))).astype(BF16)
    w_up = jnp.zeros((LANES, 2 * GLA_QK), F32)
    w_up = w_up.at[:GLA_GATE_RANK, :GLA_QK].set(p["w_gate_up_f"])
    w_up = w_up.at[GLA_GATE_RANK:2 * GLA_GATE_RANK, GLA_QK:].set(p["w_gate_up_b"])
    w["w_up"] = w_up.astype(BF16)
    w["b_g"] = jnp.concatenate([p["b_gate_f"], p["b_gate_b"]])[None, :]
    for name in ("w_out", "w_xq", "w_xkv", "w_xo", "w_ffn_gate", "w_ffn_up", "w_ffn_down"):
        w[name] = p[name].astype(BF16)
    for name in ("norm_mix_pre", "gla_norm_w", "lambda_q1", "lambda_k1", "lambda_q2", "lambda_k2",
                 "diff_subln_w", "norm_mix_post", "norm_xattn_pre", "norm_mem", "norm_xattn_post",
                 "norm_ffn_pre", "norm_ffn_post"):
        w[name] = p[name][None, :]
    return w


def _encoder_layer(x, mem, w, lam_init):
    batch, seq, _ = x.shape
    assert seq % (MIX_ROW_SPLIT * TOKEN_TILE) == 0 and seq % INPROJ_TILE == 0 and mem.shape[1] == MEM_LEN
    x2d = x.reshape(batch * seq, D_MODEL)
    mem_k, mem_v = _mem_kv(mem.reshape(batch * MEM_LEN, D_MODEL), w["norm_mem"], w["w_xkv"])
    f, b, v, og, decf, decb, dq, dk, dv = _in_proj(x2d, seq, w)
    o_f, o_b = _gla(f, b, v, decf, decb, batch, seq)
    o_diff = _diff_attn(dq, dk, dv, w, batch, seq, lam_init)
    x2 = _mix_xattn(x2d, o_f, o_b, og, o_diff, mem_k, mem_v, seq, w)
    return _ffn(x2, w).reshape(batch, seq, D_MODEL)


def kernel(x_prompt, x_sample, mem_prompt, mem_sample, norm_mix_pre, w_in, w_gate_up_f, b_gate_f,
           w_gate_up_b, b_gate_b, gla_norm_w, lambda_q1, lambda_k1, lambda_q2, lambda_k2, diff_subln_w,
           w_out, norm_mix_post, norm_xattn_pre, norm_mem, w_xq, w_xkv, w_xo, norm_xattn_post,
           norm_ffn_pre, w_ffn_gate, w_ffn_up, w_ffn_down, norm_ffn_post):
    stacked = dict(
        norm_mix_pre=norm_mix_pre, w_in=w_in, w_gate_up_f=w_gate_up_f, b_gate_f=b_gate_f,
        w_gate_up_b=w_gate_up_b, b_gate_b=b_gate_b, gla_norm_w=gla_norm_w, lambda_q1=lambda_q1,
        lambda_k1=lambda_k1, lambda_q2=lambda_q2, lambda_k2=lambda_k2, diff_subln_w=diff_subln_w,
        w_out=w_out, norm_mix_post=norm_mix_post, norm_xattn_pre=norm_xattn_pre, norm_mem=norm_mem,
        w_xq=w_xq, w_xkv=w_xkv, w_xo=w_xo, norm_xattn_post=norm_xattn_post,
        norm_ffn_pre=norm_ffn_pre, w_ffn_gate=w_ffn_gate, w_ffn_up=w_ffn_up, w_ffn_down=w_ffn_down,
        norm_ffn_post=norm_ffn_post)
    y_prompt, y_sample = x_prompt, x_sample
    for layer in range(w_in.shape[0]):
        lam_init = 0.8 - 0.6 * math.exp(-0.3 * layer)
        w = _prepare_weights({name: t[layer] for name, t in stacked.items()})
        y_prompt = _encoder_layer(y_prompt, mem_prompt, w, lam_init)
        y_sample = _encoder_layer(y_sample, mem_sample, w, lam_init)
    return (y_prompt, y_sample)
```
